```python
import jax, jax.numpy as jnp
from jax import lax
import numpy as np

D_MODEL = 1024
BATCH = 8
SEQ = 2048
DEPTH = 2

CHUNK = 64
D_MIX = D_MODEL
POOL_WINDOWS = (2, 4, 8, 16)
N_POOL_GROUPS = len(POOL_WINDOWS)
D_POOL = D_MIX // 4
POOL_GC = D_POOL // N_POOL_GROUPS
D_SGU = D_MIX // 2
SGU_HEADS = 4
SGU_HD = D_SGU // SGU_HEADS
SGU_BLOCK = 128
D_CONV = D_MIX - D_POOL - D_SGU
CONV_W = 3
D_IN = D_POOL + 2 * D_SGU + 3 * D_CONV
SPLITS = (D_POOL, D_POOL + D_SGU, D_POOL + 2 * D_SGU, D_POOL + 2 * D_SGU + D_CONV, D_POOL + 2 * D_SGU + 2 * D_CONV)
D_FF = (7 * D_MODEL) // 2
N_EXPERTS = 8
TOP_K = 2
N_DENSE = (DEPTH + 1) // 2
N_MOE = DEPTH // 2
EPS = 1e-6

kernel_name = "hybrid_pool_sgu_shortconv_moe"


def rmsnorm(x, g):
    xf = x.astype(jnp.float32)
    y = xf * lax.rsqrt(jnp.mean(xf * xf, axis=-1, keepdims=True) + EPS)
    return (y * g.astype(jnp.float32)).astype(x.dtype)


def layernorm(x, g, b):
    xf = x.astype(jnp.float32)
    mu = jnp.mean(xf, axis=-1, keepdims=True)
    var = jnp.mean(jnp.square(xf - mu), axis=-1, keepdims=True)
    y = (xf - mu) * lax.rsqrt(var + EPS)
    return (y * g.astype(jnp.float32) + b.astype(jnp.float32)).astype(x.dtype)


def swiglu(h, w_gate, w_up, w_down):
    return (jax.nn.silu(h @ w_gate) * (h @ w_up)) @ w_down


def pool_mixer(a, pool_w, pool_scale):
    bsz, seq, _ = a.shape
    af = a.astype(jnp.float32).reshape(bsz, seq, N_POOL_GROUPS, POOL_GC)
    cs = jnp.cumsum(af, axis=1)
    pos = jnp.arange(seq)
    outs = []
    for g, w in enumerate(POOL_WINDOWS):
        c = cs[:, :, g]
        c_prev = jnp.pad(c, ((0, 0), (w, 0), (0, 0)))[:, :seq]
        count = jnp.minimum(pos + 1, w).astype(jnp.float32)[None, :, None]
        outs.append((c - c_prev) / count - af[:, :, g])
    d = jnp.stack(outs, axis=2).astype(a.dtype)
    y = jnp.einsum("bsgc,gcd->bsgd", d, pool_w)
    return y.reshape(bsz, seq, D_POOL) * pool_scale


def sgu_mixer(u, v, ln_g, ln_b, w_s, b_s):
    bsz, seq, _ = u.shape
    n_blk = seq // SGU_BLOCK
    vn = layernorm(v, ln_g, ln_b).reshape(bsz, n_blk, SGU_BLOCK, SGU_HEADS, SGU_HD)
    cid = jnp.arange(SGU_BLOCK) // CHUNK
    mask = (cid[:, None] >= cid[None, :]).astype(w_s.dtype)
    mixed = jnp.einsum("hij,bnjhc->bnihc", w_s * mask, vn) + b_s.T[None, None, :, :, None]
    return u * mixed.reshape(bsz, seq, D_SGU)


def short_conv_mixer(gb, gc, xc, conv_w):
    z = gc * xc
    zc = lax.conv_general_dilated(
        z, conv_w[:, None, :], window_strides=(1,), padding=[(CONV_W - 1, 0)],
        dimension_numbers=("NWC", "WIO", "NWC"), feature_group_count=D_CONV)
    return gb * zc


def token_mixer(h, w_in, pool_w, pool_scale, sgu_ln_g, sgu_ln_b, sgu_w, sgu_b, conv_w, group_g, w_out):
    p = h @ w_in
    a, u, v, gb, gc, xc = jnp.split(p, list(SPLITS), axis=-1)
    y_a = pool_mixer(a, pool_w, pool_scale)
    y_b = sgu_mixer(u, v, sgu_ln_g, sgu_ln_b, sgu_w, sgu_b)
    y_c = short_conv_mixer(gb, gc, xc, conv_w)
    y = jnp.concatenate([
        rmsnorm(y_a, group_g[:D_POOL]),
        rmsnorm(y_b, group_g[D_POOL:D_POOL + D_SGU]),
        rmsnorm(y_c, group_g[D_POOL + D_SGU:]),
    ], axis=-1)
    return y @ w_out


def moe_ffn(h, router_w, router_b, w_gate, w_up, w_down):
    bsz, seq, d = h.shape
    t = h.reshape(bsz * seq, d)
    logits = t.astype(jnp.float32) @ router_w.astype(jnp.float32) + router_b.astype(jnp.float32)
    top_v, top_i = lax.top_k(logits, TOP_K)
    gates = jax.nn.softmax(top_v, axis=-1)
    combine = jnp.sum(jax.nn.one_hot(top_i, N_EXPERTS, dtype=jnp.float32) * gates[..., None], axis=1)
    out = jnp.zeros((bsz * seq, d), jnp.float32)
    for e in range(N_EXPERTS):
        y_e = swiglu(t, w_gate[e], w_up[e], w_down[e])
        out = out + combine[:, e:e + 1] * y_e.astype(jnp.float32)
    return out.astype(h.dtype).reshape(bsz, seq, d)


def setup_inputs(seed: int = 0) -> dict:
    key = jax.random.key(seed)
    ks = jax.random.split(key, 24)

    def nrm(k, shape, scale):
        return jax.random.normal(k, shape, jnp.float32) * scale

    return {
        "x": nrm(ks[0], (BATCH, SEQ, D_MODEL), 1.0),
        "norm1_g": 1.0 + nrm(ks[1], (DEPTH, D_MODEL), 0.05),
        "w_in": nrm(ks[2], (DEPTH, D_MODEL, D_IN), D_MODEL ** -0.5),
        "pool_w": nrm(ks[3], (DEPTH, N_POOL_GROUPS, POOL_GC, POOL_GC), POOL_GC ** -0.5),
        "pool_scale": 1.0 + nrm(ks[4], (DEPTH, D_POOL), 0.1),
        "sgu_ln_g": 1.0 + nrm(ks[5], (DEPTH, D_SGU), 0.05),
        "sgu_ln_b": nrm(ks[6], (DEPTH, D_SGU), 0.02),
        "sgu_w": nrm(ks[7], (DEPTH, SGU_HEADS, SGU_BLOCK, SGU_BLOCK), SGU_BLOCK ** -0.5),
        "sgu_b": 1.0 + nrm(ks[8], (DEPTH, SGU_HEADS, SGU_BLOCK), 0.05),
        "conv_w": nrm(ks[9], (DEPTH, CONV_W, D_CONV), CONV_W ** -0.5),
        "group_g": 1.0 + nrm(ks[10], (DEPTH, D_MIX), 0.05),
        "w_out": nrm(ks[11], (DEPTH, D_MIX, D_MODEL), D_MIX ** -0.5),
        "norm2_g": 1.0 + nrm(ks[12], (DEPTH, D_MODEL), 0.05),
        "ffn_w_gate": nrm(ks[13], (N_DENSE, D_MODEL, D_FF), D_MODEL ** -0.5),
        "ffn_w_up": nrm(ks[14], (N_DENSE, D_MODEL, D_FF), D_MODEL ** -0.5),
        "ffn_w_down": nrm(ks[15], (N_DENSE, D_FF, D_MODEL), D_FF ** -0.5),
        "router_w": nrm(ks[16], (N_MOE, D_MODEL, N_EXPERTS), D_MODEL ** -0.5),
        "router_b": nrm(ks[17], (N_MOE, N_EXPERTS), 0.01),
        "moe_w_gate": nrm(ks[18], (N_MOE, N_EXPERTS, D_MODEL, D_FF), D_MODEL ** -0.5),
        "moe_w_up": nrm(ks[19], (N_MOE, N_EXPERTS, D_MODEL, D_FF), D_MODEL ** -0.5),
        "moe_w_down": nrm(ks[20], (N_MOE, N_EXPERTS, D_FF, D_MODEL), D_FF ** -0.5),
        "final_g": 1.0 + nrm(ks[21], (D_MODEL,), 0.05),
    }


def reference(x, norm1_g, w_in, pool_w, pool_scale, sgu_ln_g, sgu_ln_b, sgu_w, sgu_b, conv_w,
              group_g, w_out, norm2_g, ffn_w_gate, ffn_w_up, ffn_w_down, router_w, router_b,
              moe_w_gate, moe_w_up, moe_w_down, final_g):
    for l in range(DEPTH):
        h = rmsnorm(x, norm1_g[l])
        x = x + token_mixer(h, w_in[l], pool_w[l], pool_scale[l], sgu_ln_g[l], sgu_ln_b[l],
                            sgu_w[l], sgu_b[l], conv_w[l], group_g[l], w_out[l])
        h = rmsnorm(x, norm2_g[l])
        j = l // 2
        if l % 2 == 0:
            x = x + swiglu(h, ffn_w_gate[j], ffn_w_up[j], ffn_w_down[j])
        else:
            x = x + moe_ffn(h, router_w[j], router_b[j], moe_w_gate[j], moe_w_up[j], moe_w_down[j])
    return rmsnorm(x, final_g)
```

```python
import functools

import jax
import jax.numpy as jnp
from jax import lax
from jax.experimental import pallas as pl
from jax.experimental.pallas import tpu as pltpu

D_MODEL = 1024
SEQ = 2048
DEPTH = 2
POOL_WINDOWS = (2, 4, 8, 16)
POOL_GC = 64
D_POOL = 256
D_SGU = 512
SGU_HEADS = 4
SGU_HD = 128
SGU_BLOCK = 128
CHUNK = 64
D_CONV = 256
CONV_W = 3
D_IN = 2048
D_FF = 3584
N_EXPERTS = 8
EPS = 1e-6

LANES = 128
POOL_HALO = 16
CONV_HALO = 8
TM_MIX = 512
TM_FFN = 1024
TF_FFN = 512
TM_ROUTE = 512
VMEM_LIMIT = 48 * 1024 * 1024

C_A, C_U, C_V, C_GB, C_GC, C_XC = 0, 256, 768, 1280, 1536, 1792

F32 = jnp.float32
BF16 = jnp.bfloat16


def _rms(x, g):
    ms = jnp.mean(x * x, axis=-1, keepdims=True)
    return (x * lax.rsqrt(ms + EPS)) * g


def _mixer_kernel(x_ref, n1g_ref, win_ref, poolw_ref, pscale_ref, lng_ref, lnb_ref, sguw_ref,
                  sgub_ref, convw_ref, gg_ref, wout_ref, o_ref, a_scr, z_scr, y_scr):
    tiles_per_seq = SEQ // TM_MIX
    seq_tile = pl.program_id(0) % tiles_per_seq

    @pl.when(seq_tile == 0)
    def _():
        a_scr[0:POOL_HALO, :] = jnp.zeros((POOL_HALO, D_POOL), F32)
        z_scr[0:CONV_HALO, :] = jnp.zeros((CONV_HALO, D_CONV), F32)

    x = x_ref[...]
    hb = _rms(x, n1g_ref[...]).astype(BF16)

    def proj(c0, width):
        return jnp.dot(hb, win_ref[:, c0:c0 + width], preferred_element_type=F32)

    a = proj(C_A, D_POOL)
    a_scr[POOL_HALO:POOL_HALO + TM_MIX, :] = a
    lane = lax.broadcasted_iota(jnp.int32, (1, D_POOL), 1)
    group = lane // POOL_GC
    acc = a
    wsum = jnp.zeros_like(a)
    shift = 1
    for g, w in enumerate(POOL_WINDOWS):
        while shift < w:
            acc = acc + a_scr[POOL_HALO - shift:POOL_HALO - shift + TM_MIX, :]
            shift += 1
        wsum = jnp.where(group == g, acc, wsum)
    wlane = jnp.where(group == 0, POOL_WINDOWS[0],
                      jnp.where(group == 1, POOL_WINDOWS[1],
                                jnp.where(group == 2, POOL_WINDOWS[2], POOL_WINDOWS[3])))
    pos = seq_tile * TM_MIX + lax.broadcasted_iota(jnp.int32, (TM_MIX, 1), 0)
    count = jnp.minimum(pos + 1, wlane).astype(F32)
    d = wsum / count - a
    y_a = jnp.dot(d.astype(BF16), poolw_ref[...], preferred_element_type=F32) * pscale_ref[...]
    a_scr[0:POOL_HALO, :] = a_scr[TM_MIX:TM_MIX + POOL_HALO, :]
    gg = gg_ref[...]
    y_scr[:, 0:D_POOL] = _rms(y_a, gg[:, 0:D_POOL]).astype(BF16)

    gb = proj(C_GB, D_CONV)
    z = proj(C_GC, D_CONV) * proj(C_XC, D_CONV)
    z_scr[CONV_HALO:CONV_HALO + TM_MIX, :] = z
    cw = convw_ref[...]
    zc = (cw[0:1, :] * z_scr[CONV_HALO - 2:CONV_HALO - 2 + TM_MIX, :]
          + cw[1:2, :] * z_scr[CONV_HALO - 1:CONV_HALO - 1 + TM_MIX, :]
          + cw[2:3, :] * z)
    y_c = gb * zc
    z_scr[0:CONV_HALO, :] = z_scr[TM_MIX:TM_MIX + CONV_HALO, :]
    y_scr[:, D_POOL + D_SGU:] = _rms(y_c, gg[:, D_POOL + D_SGU:]).astype(BF16)

    u = proj(C_U, D_SGU)
    v = proj(C_V, D_SGU)
    mu = jnp.mean(v, axis=-1, keepdims=True)
    vc = v - mu
    var = jnp.mean(vc * vc, axis=-1, keepdims=True)
    vn = ((vc * lax.rsqrt(var + EPS)) * lng_ref[...] + lnb_ref[...]).astype(BF16)
    ci = lax.broadcasted_iota(jnp.int32, (SGU_BLOCK, SGU_BLOCK), 0) // CHUNK
    cj = lax.broadcasted_iota(jnp.int32, (SGU_BLOCK, SGU_BLOCK), 1) // CHUNK
    mask = (ci >= cj).astype(F32)
    sgub = sgub_ref[...]
    head_cols = []
    for hd in range(SGU_HEADS):
        wm = (sguw_ref[hd] * mask).astype(BF16)
        bias = sgub[:, hd:hd + 1]
        blocks = []
        for blk in range(TM_MIX // SGU_BLOCK):
            vblk = vn[blk * SGU_BLOCK:(blk + 1) * SGU_BLOCK, hd * SGU_HD:(hd + 1) * SGU_HD]
            blocks.append(jnp.dot(wm, vblk, preferred_element_type=F32) + bias)
        head_cols.append(jnp.concatenate(blocks, axis=0))
    mixed = jnp.concatenate(head_cols, axis=1)
    y_b = u * mixed
    y_scr[:, D_POOL:D_POOL + D_SGU] = _rms(y_b, gg[:, D_POOL:D_POOL + D_SGU]).astype(BF16)

    o_ref[...] = x + jnp.dot(y_scr[...], wout_ref[...], preferred_element_type=F32)


def _token_mixer(x2d, n1g, win, poolw_bd, pscale, lng, lnb, sguw, sgub_t, convw, gg, wout):
    n_tok = x2d.shape[0]
    const = lambda *shape: pl.BlockSpec(shape, lambda i: (0,) * len(shape))
    return pl.pallas_call(
        _mixer_kernel,
        grid=(n_tok // TM_MIX,),
        in_specs=[
            pl.BlockSpec((TM_MIX, D_MODEL), lambda i: (i, 0)),
            const(1, D_MODEL),
            const(D_MODEL, D_IN),
            const(D_POOL, D_POOL),
            const(1, D_POOL),
            const(1, D_SGU),
            const(1, D_SGU),
            const(SGU_HEADS, SGU_BLOCK, SGU_BLOCK),
            const(SGU_BLOCK, SGU_HEADS),
            const(CONV_W, D_CONV),
            const(1, D_MODEL),
            const(D_MODEL, D_MODEL),
        ],
        out_specs=pl.BlockSpec((TM_MIX, D_MODEL), lambda i: (i, 0)),
        out_shape=jax.ShapeDtypeStruct((n_tok, D_MODEL), F32),
        scratch_shapes=[
            pltpu.VMEM((POOL_HALO + TM_MIX, D_POOL), F32),
            pltpu.VMEM((CONV_HALO + TM_MIX, D_CONV), F32),
            pltpu.VMEM((TM_MIX, D_MODEL), BF16),
        ],
        compiler_params=pltpu.CompilerParams(
            dimension_semantics=("arbitrary",), vmem_limit_bytes=VMEM_LIMIT),
        name="token_mixer",
    )(x2d, n1g, win, poolw_bd, pscale, lng, lnb, sguw, sgub_t, convw, gg, wout)


def _swiglu_step(hb, wg_ref, wu_ref, wd_ref):
    g = jnp.dot(hb, wg_ref[...].astype(BF16), preferred_element_type=F32)
    u = jnp.dot(hb, wu_ref[...].astype(BF16), preferred_element_type=F32)
    act = (g * jax.nn.sigmoid(g)) * u
    return jnp.dot(act.astype(BF16), wd_ref[...].astype(BF16), preferred_element_type=F32)


def _ffn_kernel(x_ref, n2g_ref, wg_ref, wu_ref, wd_ref, o_ref, hb_scr, acc_scr):
    j = pl.program_id(1)

    @pl.when(j == 0)
    def _():
        x = x_ref[...]
        hb_scr[...] = _rms(x, n2g_ref[...]).astype(BF16)
        acc_scr[...] = x

    acc_scr[...] += _swiglu_step(hb_scr[...], wg_ref, wu_ref, wd_ref)

    @pl.when(j == pl.num_programs(1) - 1)
    def _():
        o_ref[...] = acc_scr[...]


def _dense_ffn(x2d, n2g, wg, wu, wd):
    n_tok = x2d.shape[0]
    return pl.pallas_call(
        _ffn_kernel,
        grid=(n_tok // TM_FFN, D_FF // TF_FFN),
        in_specs=[
            pl.BlockSpec((TM_FFN, D_MODEL), lambda i, j: (i, 0)),
            pl.BlockSpec((1, D_MODEL), lambda i, j: (0, 0)),
            pl.BlockSpec((D_MODEL, TF_FFN), lambda i, j: (0, j)),
            pl.BlockSpec((D_MODEL, TF_FFN), lambda i, j: (0, j)),
            pl.BlockSpec((TF_FFN, D_MODEL), lambda i, j: (j, 0)),
        ],
        out_specs=pl.BlockSpec((TM_FFN, D_MODEL), lambda i, j: (i, 0)),
        out_shape=jax.ShapeDtypeStruct((n_tok, D_MODEL), F32),
        scratch_shapes=[pltpu.VMEM((TM_FFN, D_MODEL), BF16), pltpu.VMEM((TM_FFN, D_MODEL), F32)],
        compiler_params=pltpu.CompilerParams(
            dimension_semantics=("arbitrary", "arbitrary"), vmem_limit_bytes=VMEM_LIMIT),
        name="dense_ffn",
    )(x2d, n2g, wg, wu, wd)


def _router_kernel(x_ref, n2g_ref, rw_ref, rb_ref, h_ref, comb_ref):
    hb = _rms(x_ref[...], n2g_ref[...]).astype(BF16)
    h_ref[...] = hb
    logits = jnp.dot(hb, rw_ref[...], preferred_element_type=F32) + rb_ref[...]
    lane = lax.broadcasted_iota(jnp.int32, logits.shape, 1).astype(F32)
    m1 = jnp.max(logits, axis=-1, keepdims=True)
    i1 = jnp.min(jnp.where(logits == m1, lane, float(LANES)), axis=-1, keepdims=True)
    rest = jnp.where(lane == i1, -jnp.inf, logits)
    m2 = jnp.max(rest, axis=-1, keepdims=True)
    i2 = jnp.min(jnp.where(rest == m2, lane, float(LANES)), axis=-1, keepdims=True)
    e2 = jnp.exp(m2 - m1)
    denom = 1.0 + e2
    comb_ref[...] = jnp.where(lane == i1, 1.0 / denom, 0.0) + jnp.where(lane == i2, e2 / denom, 0.0)


def _router(x2d, n2g, rw_pad, rb_pad):
    n_tok = x2d.shape[0]
    return pl.pallas_call(
        _router_kernel,
        grid=(n_tok // TM_ROUTE,),
        in_specs=[
            pl.BlockSpec((TM_ROUTE, D_MODEL), lambda i: (i, 0)),
            pl.BlockSpec((1, D_MODEL), lambda i: (0, 0)),
            pl.BlockSpec((D_MODEL, LANES), lambda i: (0, 0)),
            pl.BlockSpec((1, LANES), lambda i: (0, 0)),
        ],
        out_specs=[
            pl.BlockSpec((TM_ROUTE, D_MODEL), lambda i: (i, 0)),
            pl.BlockSpec((TM_ROUTE, LANES), lambda i: (i, 0)),
        ],
        out_shape=[
            jax.ShapeDtypeStruct((n_tok, D_MODEL), BF16),
            jax.ShapeDtypeStruct((n_tok, LANES), F32),
        ],
        compiler_params=pltpu.CompilerParams(
            dimension_semantics=("arbitrary",), vmem_limit_bytes=VMEM_LIMIT),
        name="router",
    )(x2d, n2g, rw_pad, rb_pad)


def _moe_kernel(x_ref, h_ref, comb_ref, wg_ref, wu_ref, wd_ref, fg_ref, o_ref, acc_scr):
    e = pl.program_id(1)
    j = pl.program_id(2)

    @pl.when((e == 0) & (j == 0))
    def _():
        acc_scr[...] = jnp.zeros_like(acc_scr)

    comb = comb_ref[...]
    lane = lax.broadcasted_iota(jnp.int32, comb.shape, 1)
    c_e = jnp.sum(jnp.where(lane == e, comb, 0.0), axis=-1, keepdims=True)
    y = _swiglu_step(h_ref[...], wg_ref.at[0], wu_ref.at[0], wd_ref.at[0])
    acc_scr[...] += c_e * y

    @pl.when((e == pl.num_programs(1) - 1) & (j == pl.num_programs(2) - 1))
    def _():
        o_ref[...] = _rms(x_ref[...] + acc_scr[...], fg_ref[...])


def _moe_ffn(x2d, hb, comb, wg, wu, wd, fg):
    n_tok = x2d.shape[0]
    return pl.pallas_call(
        _moe_kernel,
        grid=(n_tok // TM_FFN, N_EXPERTS, D_FF // TF_FFN),
        in_specs=[
            pl.BlockSpec((TM_FFN, D_MODEL), lambda i, e, j: (i, 0)),
            pl.BlockSpec((TM_FFN, D_MODEL), lambda i, e, j: (i, 0)),
            pl.BlockSpec((TM_FFN, LANES), lambda i, e, j: (i, 0)),
            pl.BlockSpec((1, D_MODEL, TF_FFN), lambda i, e, j: (e, 0, j)),
            pl.BlockSpec((1, D_MODEL, TF_FFN), lambda i, e, j: (e, 0, j)),
            pl.BlockSpec((1, TF_FFN, D_MODEL), lambda i, e, j: (e, j, 0)),
            pl.BlockSpec((1, D_MODEL), lambda i, e, j: (0, 0)),
        ],
        out_specs=pl.BlockSpec((TM_FFN, D_MODEL), lambda i, e, j: (i, 0)),
        out_shape=jax.ShapeDtypeStruct((n_tok, D_MODEL), F32),
        scratch_shapes=[pltpu.VMEM((TM_FFN, D_MODEL), F32)],
        compiler_params=pltpu.CompilerParams(
            dimension_semantics=("arbitrary", "arbitrary", "arbitrary"),
            vmem_limit_bytes=VMEM_LIMIT),
        name="moe_ffn",
    )(x2d, hb, comb, wg, wu, wd, fg)


def _block_diag(blocks):
    n, r, c = blocks.shape
    eye = jnp.eye(n, dtype=blocks.dtype)
    return (eye[:, None, :, None] * blocks[:, :, None, :]).reshape(n * r, n * c)


def kernel(x, norm1_g, w_in, pool_w, pool_scale, sgu_ln_g, sgu_ln_b, sgu_w, sgu_b, conv_w,
           group_g, w_out, norm2_g, ffn_w_gate, ffn_w_up, ffn_w_down, router_w, router_b,
           moe_w_gate, moe_w_up, moe_w_down, final_g):
    bsz, seq, d = x.shape
    assert (seq, d) == (SEQ, D_MODEL) and DEPTH == 2
    x2d = x.reshape(bsz * seq, d)
    row = lambda t: t.reshape(1, -1)

    def mixer(l, xin):
        return _token_mixer(
            xin, row(norm1_g[l]), w_in[l].astype(BF16), _block_diag(pool_w[l]).astype(BF16),
            row(pool_scale[l]), row(sgu_ln_g[l]), row(sgu_ln_b[l]), sgu_w[l], sgu_b[l].T,
            conv_w[l], row(group_g[l]), w_out[l].astype(BF16))

    x2d = mixer(0, x2d)
    x2d = _dense_ffn(x2d, row(norm2_g[0]), ffn_w_gate[0], ffn_w_up[0], ffn_w_down[0])
    x2d = mixer(1, x2d)
    rw_pad = jnp.zeros((d, LANES), BF16).at[:, :N_EXPERTS].set(router_w[0].astype(BF16))
    rb_pad = jnp.full((1, LANES), -1e30, F32).at[0, :N_EXPERTS].set(router_b[0])
    hb, comb = _router(x2d, row(norm2_g[1]), rw_pad, rb_pad)
    out = _moe_ffn(x2d, hb, comb, moe_w_gate[0], moe_w_up[0], moe_w_down[0], row(final_g))
    return out.reshape(bsz, seq, d)
```

```python
import functools

import jax
import jax.numpy as jnp
from jax import lax
from jax.experimental import pallas as pl
from jax.experimental.pallas import tpu as pltpu

D_MODEL = 1024
SEQ = 2048
DEPTH = 2
POOL_WINDOWS = (2, 4, 8, 16)
POOL_GC = 64
D_POOL = 256
D_SGU = 512
SGU_HEADS = 4
SGU_HD = 128
SGU_BLOCK = 128
CHUNK = 64
D_CONV = 256
CONV_W = 3
D_IN = 2048
D_FF = 3584
N_EXPERTS = 8
EPS = 1e-6

LANES = 128
POOL_HALO = 16
CONV_HALO = 8
TM_MIX = 512
TM_FFN = 1024
TF_FFN = 512
TM_ROUTE = 512
TM_EXP = 1024
TM_MOVE = 256
TOP_K = 2
VMEM_LIMIT = 48 * 1024 * 1024
M_I1, M_I2, M_R1, M_R2, M_G1, M_G2 = 0, 1, 2, 3, 4, 5

C_A, C_U, C_V, C_GB, C_GC, C_XC = 0, 256, 768, 1280, 1536, 1792

F32 = jnp.float32
BF16 = jnp.bfloat16


def _rms(x, g):
    ms = jnp.mean(x * x, axis=-1, keepdims=True)
    return (x * lax.rsqrt(ms + EPS)) * g


def _mixer_kernel(x_ref, n1g_ref, win_ref, poolw_ref, pscale_ref, lng_ref, lnb_ref, sguw_ref,
                  sgub_ref, convw_ref, gg_ref, wout_ref, o_ref, a_scr, z_scr, y_scr):
    tiles_per_seq = SEQ // TM_MIX
    seq_tile = pl.program_id(0) % tiles_per_seq

    @pl.when(seq_tile == 0)
    def _():
        a_scr[0:POOL_HALO, :] = jnp.zeros((POOL_HALO, D_POOL), F32)
        z_scr[0:CONV_HALO, :] = jnp.zeros((CONV_HALO, D_CONV), F32)

    x = x_ref[...]
    hb = _rms(x, n1g_ref[...]).astype(BF16)

    def proj(c0, width):
        return jnp.dot(hb, win_ref[:, c0:c0 + width], preferred_element_type=F32)

    a = proj(C_A, D_POOL)
    a_scr[POOL_HALO:POOL_HALO + TM_MIX, :] = a
    lane = lax.broadcasted_iota(jnp.int32, (1, D_POOL), 1)
    group = lane // POOL_GC
    acc = a
    wsum = jnp.zeros_like(a)
    shift = 1
    for g, w in enumerate(POOL_WINDOWS):
        while shift < w:
            acc = acc + a_scr[POOL_HALO - shift:POOL_HALO - shift + TM_MIX, :]
            shift += 1
        wsum = jnp.where(group == g, acc, wsum)
    wlane = jnp.where(group == 0, POOL_WINDOWS[0],
                      jnp.where(group == 1, POOL_WINDOWS[1],
                                jnp.where(group == 2, POOL_WINDOWS[2], POOL_WINDOWS[3])))
    pos = seq_tile * TM_MIX + lax.broadcasted_iota(jnp.int32, (TM_MIX, 1), 0)
    count = jnp.minimum(pos + 1, wlane).astype(F32)
    d = wsum / count - a
    y_a = jnp.dot(d.astype(BF16), poolw_ref[...], preferred_element_type=F32) * pscale_ref[...]
    a_scr[0:POOL_HALO, :] = a_scr[TM_MIX:TM_MIX + POOL_HALO, :]
    gg = gg_ref[...]
    y_scr[:, 0:D_POOL] = _rms(y_a, gg[:, 0:D_POOL]).astype(BF16)

    gb = proj(C_GB, D_CONV)
    z = proj(C_GC, D_CONV) * proj(C_XC, D_CONV)
    z_scr[CONV_HALO:CONV_HALO + TM_MIX, :] = z
    cw = convw_ref[...]
    zc = (cw[0:1, :] * z_scr[CONV_HALO - 2:CONV_HALO - 2 + TM_MIX, :]
          + cw[1:2, :] * z_scr[CONV_HALO - 1:CONV_HALO - 1 + TM_MIX, :]
          + cw[2:3, :] * z)
    y_c = gb * zc
    z_scr[0:CONV_HALO, :] = z_scr[TM_MIX:TM_MIX + CONV_HALO, :]
    y_scr[:, D_POOL + D_SGU:] = _rms(y_c, gg[:, D_POOL + D_SGU:]).astype(BF16)

    u = proj(C_U, D_SGU)
    v = proj(C_V, D_SGU)
    mu = jnp.mean(v, axis=-1, keepdims=True)
    vc = v - mu
    var = jnp.mean(vc * vc, axis=-1, keepdims=True)
    vn = ((vc * lax.rsqrt(var + EPS)) * lng_ref[...] + lnb_ref[...]).astype(BF16)
    ci = lax.broadcasted_iota(jnp.int32, (SGU_BLOCK, SGU_BLOCK), 0) // CHUNK
    cj = lax.broadcasted_iota(jnp.int32, (SGU_BLOCK, SGU_BLOCK), 1) // CHUNK
    mask = (ci >= cj).astype(F32)
    sgub = sgub_ref[...]
    head_cols = []
    for hd in range(SGU_HEADS):
        wm = (sguw_ref[hd] * mask).astype(BF16)
        bias = sgub[:, hd:hd + 1]
        blocks = []
        for blk in range(TM_MIX // SGU_BLOCK):
            vblk = vn[blk * SGU_BLOCK:(blk + 1) * SGU_BLOCK, hd * SGU_HD:(hd + 1) * SGU_HD]
            blocks.append(jnp.dot(wm, vblk, preferred_element_type=F32) + bias)
        head_cols.append(jnp.concatenate(blocks, axis=0))
    mixed = jnp.concatenate(head_cols, axis=1)
    y_b = u * mixed
    y_scr[:, D_POOL:D_POOL + D_SGU] = _rms(y_b, gg[:, D_POOL:D_POOL + D_SGU]).astype(BF16)

    o_ref[...] = x + jnp.dot(y_scr[...], wout_ref[...], preferred_element_type=F32)


def _token_mixer(x2d, n1g, win, poolw_bd, pscale, lng, lnb, sguw, sgub_t, convw, gg, wout):
    n_tok = x2d.shape[0]
    const = lambda *shape: pl.BlockSpec(shape, lambda i: (0,) * len(shape))
    return pl.pallas_call(
        _mixer_kernel,
        grid=(n_tok // TM_MIX,),
        in_specs=[
            pl.BlockSpec((TM_MIX, D_MODEL), lambda i: (i, 0)),
            const(1, D_MODEL),
            const(D_MODEL, D_IN),
            const(D_POOL, D_POOL),
            const(1, D_POOL),
            const(1, D_SGU),
            const(1, D_SGU),
            const(SGU_HEADS, SGU_BLOCK, SGU_BLOCK),
            const(SGU_BLOCK, SGU_HEADS),
            const(CONV_W, D_CONV),
            const(1, D_MODEL),
            const(D_MODEL, D_MODEL),
        ],
        out_specs=pl.BlockSpec((TM_MIX, D_MODEL), lambda i: (i, 0)),
        out_shape=jax.ShapeDtypeStruct((n_tok, D_MODEL), F32),
        scratch_shapes=[
            pltpu.VMEM((POOL_HALO + TM_MIX, D_POOL), F32),
            pltpu.VMEM((CONV_HALO + TM_MIX, D_CONV), F32),
            pltpu.VMEM((TM_MIX, D_MODEL), BF16),
        ],
        compiler_params=pltpu.CompilerParams(
            dimension_semantics=("arbitrary",), vmem_limit_bytes=VMEM_LIMIT),
        name="token_mixer",
    )(x2d, n1g, win, poolw_bd, pscale, lng, lnb, sguw, sgub_t, convw, gg, wout)


def _swiglu_step(hb, wg_ref, wu_ref, wd_ref):
    g = jnp.dot(hb, wg_ref[...].astype(BF16), preferred_element_type=F32)
    u = jnp.dot(hb, wu_ref[...].astype(BF16), preferred_element_type=F32)
    act = (g * jax.nn.sigmoid(g)) * u
    return jnp.dot(act.astype(BF16), wd_ref[...].astype(BF16), preferred_element_type=F32)


def _ffn_kernel(x_ref, n2g_ref, wg_ref, wu_ref, wd_ref, o_ref, hb_scr, acc_scr):
    j = pl.program_id(1)

    @pl.when(j == 0)
    def _():
        x = x_ref[...]
        hb_scr[...] = _rms(x, n2g_ref[...]).astype(BF16)
        acc_scr[...] = x

    acc_scr[...] += _swiglu_step(hb_scr[...], wg_ref, wu_ref, wd_ref)

    @pl.when(j == pl.num_programs(1) - 1)
    def _():
        o_ref[...] = acc_scr[...]


def _dense_ffn(x2d, n2g, wg, wu, wd):
    n_tok = x2d.shape[0]
    return pl.pallas_call(
        _ffn_kernel,
        grid=(n_tok // TM_FFN, D_FF // TF_FFN),
        in_specs=[
            pl.BlockSpec((TM_FFN, D_MODEL), lambda i, j: (i, 0)),
            pl.BlockSpec((1, D_MODEL), lambda i, j: (0, 0)),
            pl.BlockSpec((D_MODEL, TF_FFN), lambda i, j: (0, j)),
            pl.BlockSpec((D_MODEL, TF_FFN), lambda i, j: (0, j)),
            pl.BlockSpec((TF_FFN, D_MODEL), lambda i, j: (j, 0)),
        ],
        out_specs=pl.BlockSpec((TM_FFN, D_MODEL), lambda i, j: (i, 0)),
        out_shape=jax.ShapeDtypeStruct((n_tok, D_MODEL), F32),
        scratch_shapes=[pltpu.VMEM((TM_FFN, D_MODEL), BF16), pltpu.VMEM((TM_FFN, D_MODEL), F32)],
        compiler_params=pltpu.CompilerParams(
            dimension_semantics=("arbitrary", "arbitrary"), vmem_limit_bytes=VMEM_LIMIT),
        name="dense_ffn",
    )(x2d, n2g, wg, wu, wd)


def _router_kernel(x_ref, n2g_ref, rw_ref, rb_ref, meta_ref, cnt_ref):
    @pl.when(pl.program_id(0) == 0)
    def _():
        cnt_ref[...] = jnp.zeros_like(cnt_ref)

    hb = _rms(x_ref[...], n2g_ref[...]).astype(BF16)
    logits = jnp.dot(hb, rw_ref[...], preferred_element_type=F32) + rb_ref[...]
    lane = lax.broadcasted_iota(jnp.int32, logits.shape, 1).astype(F32)
    m1 = jnp.max(logits, axis=-1, keepdims=True)
    i1 = jnp.min(jnp.where(logits == m1, lane, float(LANES)), axis=-1, keepdims=True)
    rest = jnp.where(lane == i1, -jnp.inf, logits)
    m2 = jnp.max(rest, axis=-1, keepdims=True)
    i2 = jnp.min(jnp.where(rest == m2, lane, float(LANES)), axis=-1, keepdims=True)
    e2 = jnp.exp(m2 - m1)
    denom = 1.0 + e2

    sel = jnp.where((lane == i1) | (lane == i2), 1.0, 0.0)
    r = lax.broadcasted_iota(jnp.int32, (TM_ROUTE, TM_ROUTE), 0)
    c = lax.broadcasted_iota(jnp.int32, (TM_ROUTE, TM_ROUTE), 1)
    earlier = jnp.where(c < r, 1.0, 0.0).astype(BF16)
    before = jnp.dot(earlier, sel.astype(BF16), preferred_element_type=F32) + cnt_ref[...]
    rank1 = jnp.sum(jnp.where(lane == i1, before, 0.0), axis=-1, keepdims=True)
    rank2 = jnp.sum(jnp.where(lane == i2, before, 0.0), axis=-1, keepdims=True)
    cnt_ref[...] += jnp.sum(sel, axis=0, keepdims=True)

    record = jnp.zeros_like(logits)
    for k, val in ((M_I1, i1), (M_I2, i2), (M_R1, rank1), (M_R2, rank2),
                   (M_G1, 1.0 / denom), (M_G2, e2 / denom)):
        record = jnp.where(lane == k, val, record)
    meta_ref[...] = record


def _router(x2d, n2g, rw_pad, rb_pad):
    n_tok = x2d.shape[0]
    return pl.pallas_call(
        _router_kernel,
        grid=(n_tok // TM_ROUTE,),
        in_specs=[
            pl.BlockSpec((TM_ROUTE, D_MODEL), lambda i: (i, 0)),
            pl.BlockSpec((1, D_MODEL), lambda i: (0, 0)),
            pl.BlockSpec((D_MODEL, LANES), lambda i: (0, 0)),
            pl.BlockSpec((1, LANES), lambda i: (0, 0)),
        ],
        out_specs=[
            pl.BlockSpec((TM_ROUTE, LANES), lambda i: (i, 0)),
            pl.BlockSpec((1, LANES), lambda i: (0, 0)),
        ],
        out_shape=[
            jax.ShapeDtypeStruct((n_tok, LANES), F32),
            jax.ShapeDtypeStruct((1, LANES), F32),
        ],
        compiler_params=pltpu.CompilerParams(
            dimension_semantics=("arbitrary",), vmem_limit_bytes=VMEM_LIMIT),
        name="router",
    )(x2d, n2g, rw_pad, rb_pad)


def _row_copy(src_ref, src_row, dst_ref, dst_row, sem):
    return pltpu.make_async_copy(src_ref.at[pl.ds(src_row, 1)], dst_ref.at[pl.ds(dst_row, 1)], sem)


def _dispatch_kernel(zstart_ref, zflag_ref, pos_ref, x_ref, xs_ref, zero_scr, sem):
    @pl.when(pl.program_id(0) == 0)
    def _():
        zero_scr[...] = jnp.zeros_like(zero_scr)
        for e in range(2 * N_EXPERTS):
            @pl.when(zflag_ref[e] > 0)
            def _():
                start = pl.multiple_of(zstart_ref[e], TM_EXP)
                fill = pltpu.make_async_copy(zero_scr, xs_ref.at[pl.ds(start, TM_EXP)], sem.at[0])
                fill.start()
                fill.wait()

    def issue(r, carry):
        for k in range(TOP_K):
            _row_copy(x_ref, r, xs_ref, pos_ref[TOP_K * r + k], sem.at[k]).start()
        return carry

    lax.fori_loop(0, TM_MOVE, issue, 0, unroll=8)
    for k in range(TOP_K):
        pltpu.make_async_copy(x_ref, xs_ref.at[pl.ds(0, TM_MOVE)], sem.at[k]).wait()


def _dispatch(x2d, pos_flat, zstart, zflag, n_rows):
    n_tok = x2d.shape[0]
    grid_spec = pltpu.PrefetchScalarGridSpec(
        num_scalar_prefetch=2,
        grid=(n_tok // TM_MOVE,),
        in_specs=[
            pl.BlockSpec((TOP_K * TM_MOVE,), lambda i, zs, zf: (i,), memory_space=pltpu.SMEM),
            pl.BlockSpec((TM_MOVE, D_MODEL), lambda i, zs, zf: (i, 0)),
        ],
        out_specs=pl.BlockSpec(memory_space=pl.ANY),
        scratch_shapes=[pltpu.VMEM((TM_EXP, D_MODEL), F32), pltpu.SemaphoreType.DMA((TOP_K,))],
    )
    return pl.pallas_call(
        _dispatch_kernel,
        grid_spec=grid_spec,
        out_shape=jax.ShapeDtypeStruct((n_rows, D_MODEL), F32),
        compiler_params=pltpu.CompilerParams(
            dimension_semantics=("arbitrary",), vmem_limit_bytes=VMEM_LIMIT),
        name="dispatch",
    )(zstart, zflag, pos_flat, x2d)


def _expert_kernel(te_ref, tr_ref, nv_ref, xs_ref, n2g_ref, wg_ref, wu_ref, wd_ref, y_ref,
                   hb_scr, acc_scr):
    j = pl.program_id(1)
    valid = pl.program_id(0) < nv_ref[0]

    @pl.when(jnp.logical_not(valid) & (j == 0))
    def _():
        y_ref[...] = jnp.zeros_like(y_ref)

    @pl.when(valid)
    def _():
        @pl.when(j == 0)
        def _():
            hb_scr[...] = _rms(xs_ref[...], n2g_ref[...]).astype(BF16)
            acc_scr[...] = jnp.zeros_like(acc_scr)

        acc_scr[...] += _swiglu_step(hb_scr[...], wg_ref.at[0], wu_ref.at[0], wd_ref.at[0])

        @pl.when(j == pl.num_programs(1) - 1)
        def _():
            y_ref[...] = acc_scr[...]


def _expert_ffn(xs, n2g, wg, wu, wd, tile_expert, tile_row, n_valid):
    n_tiles = xs.shape[0] // TM_EXP
    n_j = D_FF // TF_FFN

    def ff_block(i, j, nv):
        return jnp.where(i < nv[0], j, n_j - 1)

    grid_spec = pltpu.PrefetchScalarGridSpec(
        num_scalar_prefetch=3,
        grid=(n_tiles, n_j),
        in_specs=[
            pl.BlockSpec((TM_EXP, D_MODEL), lambda i, j, te, tr, nv: (tr[i], 0)),
            pl.BlockSpec((1, D_MODEL), lambda i, j, te, tr, nv: (0, 0)),
            pl.BlockSpec((1, D_MODEL, TF_FFN), lambda i, j, te, tr, nv: (te[i], 0, ff_block(i, j, nv))),
            pl.BlockSpec((1, D_MODEL, TF_FFN), lambda i, j, te, tr, nv: (te[i], 0, ff_block(i, j, nv))),
            pl.BlockSpec((1, TF_FFN, D_MODEL), lambda i, j, te, tr, nv: (te[i], ff_block(i, j, nv), 0)),
        ],
        out_specs=pl.BlockSpec((TM_EXP, D_MODEL), lambda i, j, te, tr, nv: (i, 0)),
        scratch_shapes=[pltpu.VMEM((TM_EXP, D_MODEL), BF16), pltpu.VMEM((TM_EXP, D_MODEL), F32)],
    )
    return pl.pallas_call(
        _expert_kernel,
        grid_spec=grid_spec,
        out_shape=jax.ShapeDtypeStruct(xs.shape, F32),
        compiler_params=pltpu.CompilerParams(
            dimension_semantics=("arbitrary", "arbitrary"), vmem_limit_bytes=VMEM_LIMIT),
        name="expert_ffn",
    )(tile_expert, tile_row, n_valid, xs, n2g, wg, wu, wd)


def _combine_kernel(pos_ref, x_ref, meta_ref, fg_ref, y_ref, o_ref, y1_scr, y2_scr, sem):
    bufs = (y1_scr, y2_scr)

    def issue(r, carry):
        for k in range(TOP_K):
            _row_copy(y_ref, pos_ref[TOP_K * r + k], bufs[k], r, sem.at[k]).start()
        return carry

    lax.fori_loop(0, TM_MOVE, issue, 0, unroll=8)
    for k in range(TOP_K):
        pltpu.make_async_copy(y_ref.at[pl.ds(0, TM_MOVE)], bufs[k], sem.at[k]).wait()

    meta = meta_ref[...]
    moe = meta[:, M_G1:M_G1 + 1] * y1_scr[...] + meta[:, M_G2:M_G2 + 1] * y2_scr[...]
    o_ref[...] = _rms(x_ref[...] + moe, fg_ref[...])


def _combine(x2d, meta, fg, y, pos_flat):
    n_tok = x2d.shape[0]
    return pl.pallas_call(
        _combine_kernel,
        grid=(n_tok // TM_MOVE,),
        in_specs=[
            pl.BlockSpec((TOP_K * TM_MOVE,), lambda i: (i,), memory_space=pltpu.SMEM),
            pl.BlockSpec((TM_MOVE, D_MODEL), lambda i: (i, 0)),
            pl.BlockSpec((TM_MOVE, LANES), lambda i: (i, 0)),
            pl.BlockSpec((1, D_MODEL), lambda i: (0, 0)),
            pl.BlockSpec(memory_space=pl.ANY),
        ],
        out_specs=pl.BlockSpec((TM_MOVE, D_MODEL), lambda i: (i, 0)),
        out_shape=jax.ShapeDtypeStruct((n_tok, D_MODEL), F32),
        scratch_shapes=[pltpu.VMEM((TM_MOVE, D_MODEL), F32), pltpu.VMEM((TM_MOVE, D_MODEL), F32),
                        pltpu.SemaphoreType.DMA((TOP_K,))],
        compiler_params=pltpu.CompilerParams(
            dimension_semantics=("arbitrary",), vmem_limit_bytes=VMEM_LIMIT),
        name="combine",
    )(pos_flat, x2d, meta, fg, y)


def _routing_tables(meta, counts_f, n_tiles):
    counts = counts_f[0, :N_EXPERTS].astype(jnp.int32)
    padded = ((counts + TM_EXP - 1) // TM_EXP) * TM_EXP
    ends = jnp.cumsum(padded)
    starts = ends - padded
    idx = meta[:, M_I1:M_I2 + 1].astype(jnp.int32)
    rank = meta[:, M_R1:M_R2 + 1].astype(jnp.int32)
    onehot = idx[:, :, None] == jnp.arange(N_EXPERTS, dtype=jnp.int32)
    pos = rank + jnp.sum(jnp.where(onehot, starts, 0), axis=-1)
    n_valid = ends[-1] // TM_EXP
    tile = jnp.minimum(jnp.arange(n_tiles, dtype=jnp.int32), n_valid - 1)
    tile_expert = jnp.sum(ends[None, :] <= (tile * TM_EXP)[:, None], axis=-1).astype(jnp.int32)
    tile_expert = jnp.minimum(tile_expert, N_EXPERTS - 1)
    tail = jnp.arange(n_tiles - N_EXPERTS, n_tiles, dtype=jnp.int32)
    zstart = jnp.concatenate([ends - TM_EXP, tail * TM_EXP]).astype(jnp.int32)
    zflag = jnp.concatenate([padded > 0, tail >= n_valid]).astype(jnp.int32)
    return (pos.reshape(-1).astype(jnp.int32), tile_expert, tile.astype(jnp.int32),
            n_valid.reshape(1).astype(jnp.int32), zstart, zflag)


def _block_diag(blocks):
    n, r, c = blocks.shape
    eye = jnp.eye(n, dtype=blocks.dtype)
    return (eye[:, None, :, None] * blocks[:, :, None, :]).reshape(n * r, n * c)


def kernel(x, norm1_g, w_in, pool_w, pool_scale, sgu_ln_g, sgu_ln_b, sgu_w, sgu_b, conv_w,
           group_g, w_out, norm2_g, ffn_w_gate, ffn_w_up, ffn_w_down, router_w, router_b,
           moe_w_gate, moe_w_up, moe_w_down, final_g):
    bsz, seq, d = x.shape
    assert (seq, d) == (SEQ, D_MODEL) and DEPTH == 2
    x2d = x.reshape(bsz * seq, d)
    row = lambda t: t.reshape(1, -1)

    def mixer(l, xin):
        return _token_mixer(
            xin, row(norm1_g[l]), w_in[l].astype(BF16), _block_diag(pool_w[l]).astype(BF16),
            row(pool_scale[l]), row(sgu_ln_g[l]), row(sgu_ln_b[l]), sgu_w[l], sgu_b[l].T,
            conv_w[l], row(group_g[l]), w_out[l].astype(BF16))

    x2d = mixer(0, x2d)
    x2d = _dense_ffn(x2d, row(norm2_g[0]), ffn_w_gate[0], ffn_w_up[0], ffn_w_down[0])
    x2d = mixer(1, x2d)
    rw_pad = jnp.zeros((d, LANES), BF16).at[:, :N_EXPERTS].set(router_w[0].astype(BF16))
    rb_pad = jnp.full((1, LANES), -1e30, F32).at[0, :N_EXPERTS].set(router_b[0])
    n2g = row(norm2_g[1])
    meta, counts = _router(x2d, n2g, rw_pad, rb_pad)
    n_tiles = (TOP_K * bsz * seq) // TM_EXP + N_EXPERTS
    pos, tile_expert, tile_row, n_valid, zstart, zflag = _routing_tables(meta, counts, n_tiles)
    xs = _dispatch(x2d, pos, zstart, zflag, n_tiles * TM_EXP)
    y = _expert_ffn(xs, n2g, moe_w_gate[0], moe_w_up[0], moe_w_down[0], tile_expert, tile_row, n_valid)
    out = _combine(x2d, meta, row(final_g), y, pos)
    return out.reshape(bsz, seq, d)
```

```python
import functools

import jax
import jax.numpy as jnp
from jax import lax
from jax.experimental import pallas as pl
from jax.experimental.pallas import tpu as pltpu

D_MODEL = 1024
SEQ = 2048
DEPTH = 2
POOL_WINDOWS = (2, 4, 8, 16)
POOL_GC = 64
D_POOL = 256
D_SGU = 512
SGU_HEADS = 4
SGU_HD = 128
SGU_BLOCK = 128
CHUNK = 64
D_CONV = 256
CONV_W = 3
D_IN = 2048
D_FF = 3584
N_EXPERTS = 8
EPS = 1e-6

LANES = 128
POOL_HALO = 32
CONV_HALO = 8
TM_MIX = 512
TM_FFN = 1024
TF_FFN = 512
TM_ROUTE = 512
TM_EXP = 1024
TM_MOVE = 1024
TOP_K = 2
VMEM_LIMIT = 48 * 1024 * 1024
M_I1, M_I2, M_R1, M_R2, M_G1, M_G2 = 0, 1, 2, 3, 4, 5
META_ROWS = 8

C_A, C_U, C_V, C_GB, C_GC, C_XC = 0, 256, 768, 1280, 1536, 1792

F32 = jnp.float32
BF16 = jnp.bfloat16


def _rms(x, g):
    ms = jnp.mean(x * x, axis=-1, keepdims=True)
    return (x * lax.rsqrt(ms + EPS)) * g


def _mixer_kernel(x_ref, n1g_ref, win_ref, poolw_ref, pscale_ref, lng_ref, lnb_ref, sguw_ref,
                  sgub_ref, convw_ref, gg_ref, wout_ref, o_ref, p_scr, a_scr, s2_scr, s4_scr,
                  s8_scr, z_scr, y_scr):
    tiles_per_seq = SEQ // TM_MIX
    seq_tile = pl.program_id(0) % tiles_per_seq

    @pl.when(seq_tile == 0)
    def _():
        a_scr[0:POOL_HALO, :] = jnp.zeros((POOL_HALO, D_POOL), F32)
        z_scr[0:CONV_HALO, :] = jnp.zeros((CONV_HALO, D_CONV), F32)

    x = x_ref[...]
    hb = _rms(x, n1g_ref[...]).astype(BF16)

    p_scr[...] = jnp.dot(hb, win_ref[...], preferred_element_type=F32)

    def proj(c0, width):
        return p_scr[:, c0:c0 + width]

    a = proj(C_A, D_POOL)
    rows = POOL_HALO + TM_MIX
    a_scr[POOL_HALO:rows, :] = a
    levels = (a_scr, s2_scr, s4_scr, s8_scr)
    for k in range(1, len(POOL_WINDOWS)):
        w, lo = POOL_WINDOWS[k - 1] // 2, 8 * k
        prev = levels[k - 1]
        levels[k][lo:rows, :] = prev[lo:rows, :] + prev[lo - w:rows - w, :]
    w_last = POOL_WINDOWS[-1] // 2
    s_last = s8_scr[POOL_HALO:rows, :] + s8_scr[POOL_HALO - w_last:rows - w_last, :]
    lane = lax.broadcasted_iota(jnp.int32, (1, D_POOL), 1)
    group = lane // POOL_GC
    wsum = jnp.where(group == 0, s2_scr[POOL_HALO:rows, :],
                     jnp.where(group == 1, s4_scr[POOL_HALO:rows, :],
                               jnp.where(group == 2, s8_scr[POOL_HALO:rows, :], s_last)))
    wlane = jnp.where(group == 0, POOL_WINDOWS[0],
                      jnp.where(group == 1, POOL_WINDOWS[1],
                                jnp.where(group == 2, POOL_WINDOWS[2], POOL_WINDOWS[3])))
    seen = seq_tile * TM_MIX + lax.broadcasted_iota(jnp.int32, (TM_MIX, 1), 0) + 1
    inv_count = jnp.where(seen >= wlane, 1.0 / wlane.astype(F32), 1.0 / seen.astype(F32))
    d = wsum * inv_count - a
    y_a = jnp.dot(d.astype(BF16), poolw_ref[...], preferred_element_type=F32) * pscale_ref[...]
    a_scr[0:POOL_HALO, :] = a_scr[TM_MIX:rows, :]
    gg = gg_ref[...]
    y_scr[:, 0:D_POOL] = _rms(y_a, gg[:, 0:D_POOL]).astype(BF16)

    gb = proj(C_GB, D_CONV)
    z = proj(C_GC, D_CONV) * proj(C_XC, D_CONV)
    z_scr[CONV_HALO:CONV_HALO + TM_MIX, :] = z
    cw = convw_ref[...]
    zc = (cw[0:1, :] * z_scr[CONV_HALO - 2:CONV_HALO - 2 + TM_MIX, :]
          + cw[1:2, :] * z_scr[CONV_HALO - 1:CONV_HALO - 1 + TM_MIX, :]
          + cw[2:3, :] * z)
    y_c = gb * zc
    z_scr[0:CONV_HALO, :] = z_scr[TM_MIX:TM_MIX + CONV_HALO, :]
    y_scr[:, D_POOL + D_SGU:] = _rms(y_c, gg[:, D_POOL + D_SGU:]).astype(BF16)

    u = proj(C_U, D_SGU)
    v = proj(C_V, D_SGU)
    mu = jnp.mean(v, axis=-1, keepdims=True)
    vc = v - mu
    var = jnp.mean(vc * vc, axis=-1, keepdims=True)
    vn = ((vc * lax.rsqrt(var + EPS)) * lng_ref[...] + lnb_ref[...]).astype(BF16)
    ci = lax.broadcasted_iota(jnp.int32, (SGU_BLOCK, SGU_BLOCK), 0) // CHUNK
    cj = lax.broadcasted_iota(jnp.int32, (SGU_BLOCK, SGU_BLOCK), 1) // CHUNK
    mask = (ci >= cj).astype(F32)
    sgub = sgub_ref[...]
    head_cols = []
    for hd in range(SGU_HEADS):
        wm = (sguw_ref[hd] * mask).astype(BF16)
        bias = sgub[:, hd:hd + 1]
        blocks = []
        for blk in range(TM_MIX // SGU_BLOCK):
            vblk = vn[blk * SGU_BLOCK:(blk + 1) * SGU_BLOCK, hd * SGU_HD:(hd + 1) * SGU_HD]
            blocks.append(jnp.dot(wm, vblk, preferred_element_type=F32) + bias)
        head_cols.append(jnp.concatenate(blocks, axis=0))
    mixed = jnp.concatenate(head_cols, axis=1)
    y_b = u * mixed
    y_scr[:, D_POOL:D_POOL + D_SGU] = _rms(y_b, gg[:, D_POOL:D_POOL + D_SGU]).astype(BF16)

    o_ref[...] = x + jnp.dot(y_scr[...], wout_ref[...], preferred_element_type=F32)


def _token_mixer(x2d, n1g, win, poolw_bd, pscale, lng, lnb, sguw, sgub_t, convw, gg, wout):
    n_tok = x2d.shape[0]
    const = lambda *shape: pl.BlockSpec(shape, lambda i: (0,) * len(shape))
    return pl.pallas_call(
        _mixer_kernel,
        grid=(n_tok // TM_MIX,),
        in_specs=[
            pl.BlockSpec((TM_MIX, D_MODEL), lambda i: (i, 0)),
            const(1, D_MODEL),
            const(D_MODEL, D_IN),
            const(D_POOL, D_POOL),
            const(1, D_POOL),
            const(1, D_SGU),
            const(1, D_SGU),
            const(SGU_HEADS, SGU_BLOCK, SGU_BLOCK),
            const(SGU_BLOCK, SGU_HEADS),
            const(CONV_W, D_CONV),
            const(1, D_MODEL),
            const(D_MODEL, D_MODEL),
        ],
        out_specs=pl.BlockSpec((TM_MIX, D_MODEL), lambda i: (i, 0)),
        out_shape=jax.ShapeDtypeStruct((n_tok, D_MODEL), F32),
        scratch_shapes=[
            pltpu.VMEM((TM_MIX, D_IN), F32),
            pltpu.VMEM((POOL_HALO + TM_MIX, D_POOL), F32),
            pltpu.VMEM((POOL_HALO + TM_MIX, D_POOL), F32),
            pltpu.VMEM((POOL_HALO + TM_MIX, D_POOL), F32),
            pltpu.VMEM((POOL_HALO + TM_MIX, D_POOL), F32),
            pltpu.VMEM((CONV_HALO + TM_MIX, D_CONV), F32),
            pltpu.VMEM((TM_MIX, D_MODEL), BF16),
        ],
        compiler_params=pltpu.CompilerParams(
            dimension_semantics=("arbitrary",), vmem_limit_bytes=VMEM_LIMIT),
        name="token_mixer",
    )(x2d, n1g, win, poolw_bd, pscale, lng, lnb, sguw, sgub_t, convw, gg, wout)


def _swiglu_step(hb, wg_ref, wu_ref, wd_ref):
    g = jnp.dot(hb, wg_ref[...].astype(BF16), preferred_element_type=F32)
    u = jnp.dot(hb, wu_ref[...].astype(BF16), preferred_element_type=F32)
    act = (g * jax.nn.sigmoid(g)) * u
    return jnp.dot(act.astype(BF16), wd_ref[...].astype(BF16), preferred_element_type=F32)


def _ffn_kernel(x_ref, n2g_ref, wg_ref, wu_ref, wd_ref, o_ref, hb_scr):
    @pl.when(pl.program_id(1) == 0)
    def _():
        x = x_ref[...]
        hb_scr[...] = _rms(x, n2g_ref[...]).astype(BF16)
        o_ref[...] = x

    o_ref[...] += _swiglu_step(hb_scr[...], wg_ref, wu_ref, wd_ref)


def _dense_ffn(x2d, n2g, wg, wu, wd):
    n_tok = x2d.shape[0]
    return pl.pallas_call(
        _ffn_kernel,
        grid=(n_tok // TM_FFN, D_FF // TF_FFN),
        in_specs=[
            pl.BlockSpec((TM_FFN, D_MODEL), lambda i, j: (i, 0)),
            pl.BlockSpec((1, D_MODEL), lambda i, j: (0, 0)),
            pl.BlockSpec((D_MODEL, TF_FFN), lambda i, j: (0, j)),
            pl.BlockSpec((D_MODEL, TF_FFN), lambda i, j: (0, j)),
            pl.BlockSpec((TF_FFN, D_MODEL), lambda i, j: (j, 0)),
        ],
        out_specs=pl.BlockSpec((TM_FFN, D_MODEL), lambda i, j: (i, 0)),
        out_shape=jax.ShapeDtypeStruct((n_tok, D_MODEL), F32),
        scratch_shapes=[pltpu.VMEM((TM_FFN, D_MODEL), BF16)],
        compiler_params=pltpu.CompilerParams(
            dimension_semantics=("arbitrary", "arbitrary"), vmem_limit_bytes=VMEM_LIMIT),
        name="dense_ffn",
    )(x2d, n2g, wg, wu, wd)


def _router_kernel(x_ref, n2g_ref, rw_ref, rb_ref, meta_ref, meta_t_ref, cnt_ref):
    @pl.when(pl.program_id(0) == 0)
    def _():
        cnt_ref[...] = jnp.zeros_like(cnt_ref)

    hb = _rms(x_ref[...], n2g_ref[...]).astype(BF16)
    logits = jnp.dot(hb, rw_ref[...], preferred_element_type=F32) + rb_ref[...]
    lane = lax.broadcasted_iota(jnp.int32, logits.shape, 1).astype(F32)
    m1 = jnp.max(logits, axis=-1, keepdims=True)
    i1 = jnp.min(jnp.where(logits == m1, lane, float(LANES)), axis=-1, keepdims=True)
    rest = jnp.where(lane == i1, -jnp.inf, logits)
    m2 = jnp.max(rest, axis=-1, keepdims=True)
    i2 = jnp.min(jnp.where(rest == m2, lane, float(LANES)), axis=-1, keepdims=True)
    e2 = jnp.exp(m2 - m1)
    denom = 1.0 + e2

    sel = jnp.where((lane == i1) | (lane == i2), 1.0, 0.0)
    r = lax.broadcasted_iota(jnp.int32, (TM_ROUTE, TM_ROUTE), 0)
    c = lax.broadcasted_iota(jnp.int32, (TM_ROUTE, TM_ROUTE), 1)
    earlier = jnp.where(c < r, 1.0, 0.0).astype(BF16)
    before = jnp.dot(earlier, sel.astype(BF16), preferred_element_type=F32) + cnt_ref[...]
    rank1 = jnp.sum(jnp.where(lane == i1, before, 0.0), axis=-1, keepdims=True)
    rank2 = jnp.sum(jnp.where(lane == i2, before, 0.0), axis=-1, keepdims=True)
    cnt_ref[...] += jnp.sum(sel, axis=0, keepdims=True)

    record = jnp.zeros_like(logits)
    for k, val in ((M_I1, i1), (M_I2, i2), (M_R1, rank1), (M_R2, rank2),
                   (M_G1, 1.0 / denom), (M_G2, e2 / denom)):
        record = jnp.where(lane == k, val, record)
    meta_ref[...] = record
    meta_t_ref[...] = record.T[0:META_ROWS, :]


def _router(x2d, n2g, rw_pad, rb_pad):
    n_tok = x2d.shape[0]
    return pl.pallas_call(
        _router_kernel,
        grid=(n_tok // TM_ROUTE,),
        in_specs=[
            pl.BlockSpec((TM_ROUTE, D_MODEL), lambda i: (i, 0)),
            pl.BlockSpec((1, D_MODEL), lambda i: (0, 0)),
            pl.BlockSpec((D_MODEL, LANES), lambda i: (0, 0)),
            pl.BlockSpec((1, LANES), lambda i: (0, 0)),
        ],
        out_specs=[
            pl.BlockSpec((TM_ROUTE, LANES), lambda i: (i, 0)),
            pl.BlockSpec((META_ROWS, TM_ROUTE), lambda i: (0, i)),
            pl.BlockSpec((1, LANES), lambda i: (0, 0)),
        ],
        out_shape=[
            jax.ShapeDtypeStruct((n_tok, LANES), F32),
            jax.ShapeDtypeStruct((META_ROWS, n_tok), F32),
            jax.ShapeDtypeStruct((1, LANES), F32),
        ],
        compiler_params=pltpu.CompilerParams(
            dimension_semantics=("arbitrary",), vmem_limit_bytes=VMEM_LIMIT),
        name="router",
    )(x2d, n2g, rw_pad, rb_pad)


def _row_copy(src_ref, src_row, dst_ref, dst_row, sem):
    return pltpu.make_async_copy(src_ref.at[pl.ds(src_row, 1)], dst_ref.at[pl.ds(dst_row, 1)], sem)


def _dispatch_kernel(zstart_ref, zflag_ref, pos_ref, x_ref, xs_ref, zero_scr, sem):
    @pl.when(pl.program_id(0) == 0)
    def _():
        zero_scr[...] = jnp.zeros_like(zero_scr)
        for e in range(2 * N_EXPERTS):
            @pl.when(zflag_ref[e] > 0)
            def _():
                start = pl.multiple_of(zstart_ref[e], TM_EXP)
                fill = pltpu.make_async_copy(zero_scr, xs_ref.at[pl.ds(start, TM_EXP)], sem.at[0])
                fill.start()
                fill.wait()

    def issue(r, carry):
        for k in range(TOP_K):
            _row_copy(x_ref, r, xs_ref, pos_ref[TOP_K * r + k], sem.at[k]).start()
        return carry

    lax.fori_loop(0, TM_MOVE, issue, 0, unroll=8)
    for k in range(TOP_K):
        pltpu.make_async_copy(x_ref, xs_ref.at[pl.ds(0, TM_MOVE)], sem.at[k]).wait()


def _dispatch(x2d, pos_flat, zstart, zflag, n_rows):
    n_tok = x2d.shape[0]
    grid_spec = pltpu.PrefetchScalarGridSpec(
        num_scalar_prefetch=2,
        grid=(n_tok // TM_MOVE,),
        in_specs=[
            pl.BlockSpec((TOP_K * TM_MOVE,), lambda i, zs, zf: (i,), memory_space=pltpu.SMEM),
            pl.BlockSpec((TM_MOVE, D_MODEL), lambda i, zs, zf: (i, 0)),
        ],
        out_specs=pl.BlockSpec(memory_space=pl.ANY),
        scratch_shapes=[pltpu.VMEM((TM_EXP, D_MODEL), F32), pltpu.SemaphoreType.DMA((TOP_K,))],
    )
    return pl.pallas_call(
        _dispatch_kernel,
        grid_spec=grid_spec,
        out_shape=jax.ShapeDtypeStruct((n_rows, D_MODEL), F32),
        compiler_params=pltpu.CompilerParams(
            dimension_semantics=("arbitrary",), vmem_limit_bytes=VMEM_LIMIT),
        name="dispatch",
    )(zstart, zflag, pos_flat, x2d)


def _expert_kernel(te_ref, tr_ref, nv_ref, xs_ref, n2g_ref, wg_ref, wu_ref, wd_ref, y_ref,
                   hb_scr):
    @pl.when(pl.program_id(1) == 0)
    def _():
        hb_scr[...] = _rms(xs_ref[...], n2g_ref[...]).astype(BF16)
        y_ref[...] = jnp.zeros_like(y_ref)

    @pl.when(pl.program_id(0) < nv_ref[0])
    def _():
        y_ref[...] += _swiglu_step(hb_scr[...], wg_ref.at[0], wu_ref.at[0], wd_ref.at[0])


def _expert_ffn(xs, n2g, wg, wu, wd, tile_expert, tile_row, n_valid):
    n_tiles = xs.shape[0] // TM_EXP
    n_j = D_FF // TF_FFN

    def ff_block(i, j, nv):
        return jnp.where(i < nv[0], j, n_j - 1)

    grid_spec = pltpu.PrefetchScalarGridSpec(
        num_scalar_prefetch=3,
        grid=(n_tiles, n_j),
        in_specs=[
            pl.BlockSpec((TM_EXP, D_MODEL), lambda i, j, te, tr, nv: (tr[i], 0)),
            pl.BlockSpec((1, D_MODEL), lambda i, j, te, tr, nv: (0, 0)),
            pl.BlockSpec((1, D_MODEL, TF_FFN), lambda i, j, te, tr, nv: (te[i], 0, ff_block(i, j, nv))),
            pl.BlockSpec((1, D_MODEL, TF_FFN), lambda i, j, te, tr, nv: (te[i], 0, ff_block(i, j, nv))),
            pl.BlockSpec((1, TF_FFN, D_MODEL), lambda i, j, te, tr, nv: (te[i], ff_block(i, j, nv), 0)),
        ],
        out_specs=pl.BlockSpec((TM_EXP, D_MODEL), lambda i, j, te, tr, nv: (i, 0)),
        scratch_shapes=[pltpu.VMEM((TM_EXP, D_MODEL), BF16)],
    )
    return pl.pallas_call(
        _expert_kernel,
        grid_spec=grid_spec,
        out_shape=jax.ShapeDtypeStruct(xs.shape, F32),
        compiler_params=pltpu.CompilerParams(
            dimension_semantics=("arbitrary", "arbitrary"), vmem_limit_bytes=VMEM_LIMIT),
        name="expert_ffn",
    )(tile_expert, tile_row, n_valid, xs, n2g, wg, wu, wd)


def _combine_kernel(pos_ref, x_ref, meta_ref, fg_ref, y_ref, o_ref, y1_scr, y2_scr, sem):
    bufs = (y1_scr, y2_scr)

    def issue(r, carry):
        for k in range(TOP_K):
            _row_copy(y_ref, pos_ref[TOP_K * r + k], bufs[k], r, sem.at[k]).start()
        return carry

    lax.fori_loop(0, TM_MOVE, issue, 0, unroll=8)
    for k in range(TOP_K):
        pltpu.make_async_copy(y_ref.at[pl.ds(0, TM_MOVE)], bufs[k], sem.at[k]).wait()

    meta = meta_ref[...]
    moe = meta[:, M_G1:M_G1 + 1] * y1_scr[...] + meta[:, M_G2:M_G2 + 1] * y2_scr[...]
    o_ref[...] = _rms(x_ref[...] + moe, fg_ref[...])


def _combine(x2d, meta, fg, y, pos_flat):
    n_tok = x2d.shape[0]
    return pl.pallas_call(
        _combine_kernel,
        grid=(n_tok // TM_MOVE,),
        in_specs=[
            pl.BlockSpec((TOP_K * TM_MOVE,), lambda i: (i,), memory_space=pltpu.SMEM),
            pl.BlockSpec((TM_MOVE, D_MODEL), lambda i: (i, 0)),
            pl.BlockSpec((TM_MOVE, LANES), lambda i: (i, 0)),
            pl.BlockSpec((1, D_MODEL), lambda i: (0, 0)),
            pl.BlockSpec(memory_space=pl.ANY),
        ],
        out_specs=pl.BlockSpec((TM_MOVE, D_MODEL), lambda i: (i, 0)),
        out_shape=jax.ShapeDtypeStruct((n_tok, D_MODEL), F32),
        scratch_shapes=[pltpu.VMEM((TM_MOVE, D_MODEL), F32), pltpu.VMEM((TM_MOVE, D_MODEL), F32),
                        pltpu.SemaphoreType.DMA((TOP_K,))],
        compiler_params=pltpu.CompilerParams(
            dimension_semantics=("arbitrary",), vmem_limit_bytes=VMEM_LIMIT),
        name="combine",
    )(pos_flat, x2d, meta, fg, y)


def _routing_tables(meta_t, counts_f, n_tiles):
    counts = counts_f[0, :N_EXPERTS].astype(jnp.int32)
    padded = ((counts + TM_EXP - 1) // TM_EXP) * TM_EXP
    ends = jnp.cumsum(padded)
    starts = ends - padded
    idx = meta_t[M_I1:M_I2 + 1].astype(jnp.int32)
    pos = meta_t[M_R1:M_R2 + 1].astype(jnp.int32)
    for e in range(N_EXPERTS):
        pos = pos + jnp.where(idx == e, starts[e], 0)
    pos = pos.T
    n_valid = ends[-1] // TM_EXP
    tile = jnp.minimum(jnp.arange(n_tiles, dtype=jnp.int32), n_valid - 1)
    tile_expert = jnp.sum(ends[None, :] <= (tile * TM_EXP)[:, None], axis=-1).astype(jnp.int32)
    tile_expert = jnp.minimum(tile_expert, N_EXPERTS - 1)
    tail = jnp.arange(n_tiles - N_EXPERTS, n_tiles, dtype=jnp.int32)
    zstart = jnp.concatenate([ends - TM_EXP, tail * TM_EXP]).astype(jnp.int32)
    zflag = jnp.concatenate([padded > 0, tail >= n_valid]).astype(jnp.int32)
    return (pos.reshape(-1).astype(jnp.int32), tile_expert, tile.astype(jnp.int32),
            n_valid.reshape(1).astype(jnp.int32), zstart, zflag)


def _block_diag(blocks):
    n, r, c = blocks.shape
    eye = jnp.eye(n, dtype=blocks.dtype)
    return (eye[:, None, :, None] * blocks[:, :, None, :]).reshape(n * r, n * c)


def kernel(x, norm1_g, w_in, pool_w, pool_scale, sgu_ln_g, sgu_ln_b, sgu_w, sgu_b, conv_w,
           group_g, w_out, norm2_g, ffn_w_gate, ffn_w_up, ffn_w_down, router_w, router_b,
           moe_w_gate, moe_w_up, moe_w_down, final_g):
    bsz, seq, d = x.shape
    assert (seq, d) == (SEQ, D_MODEL) and DEPTH == 2
    x2d = x.reshape(bsz * seq, d)
    row = lambda t: t.reshape(1, -1)

    def mixer(l, xin):
        return _token_mixer(
            xin, row(norm1_g[l]), w_in[l].astype(BF16), _block_diag(pool_w[l]).astype(BF16),
            row(pool_scale[l]), row(sgu_ln_g[l]), row(sgu_ln_b[l]), sgu_w[l], sgu_b[l].T,
            conv_w[l], row(group_g[l]), w_out[l].astype(BF16))

    x2d = mixer(0, x2d)
    x2d = _dense_ffn(x2d, row(norm2_g[0]), ffn_w_gate[0], ffn_w_up[0], ffn_w_down[0])
    x2d = mixer(1, x2d)
    rw_pad = jnp.zeros((d, LANES), BF16).at[:, :N_EXPERTS].set(router_w[0].astype(BF16))
    rb_pad = jnp.full((1, LANES), -1e30, F32).at[0, :N_EXPERTS].set(router_b[0])
    n2g = row(norm2_g[1])
    meta, meta_t, counts = _router(x2d, n2g, rw_pad, rb_pad)
    n_tiles = (TOP_K * bsz * seq) // TM_EXP + N_EXPERTS
    pos, tile_expert, tile_row, n_valid, zstart, zflag = _routing_tables(meta_t, counts, n_tiles)
    xs = _dispatch(x2d, pos, zstart, zflag, n_tiles * TM_EXP)
    y = _expert_ffn(xs, n2g, moe_w_gate[0], moe_w_up[0], moe_w_down[0], tile_expert, tile_row, n_valid)
    out = _combine(x2d, meta, row(final_g), y, pos)
    return out.reshape(bsz, seq, d)
```

```python
import functools

import jax
import jax.numpy as jnp
from jax import lax
from jax.experimental import pallas as pl
from jax.experimental.pallas import tpu as pltpu

D_MODEL = 1024
SEQ = 2048
DEPTH = 2
POOL_WINDOWS = (2, 4, 8, 16)
POOL_GC = 64
D_POOL = 256
D_SGU = 512
SGU_HEADS = 4
SGU_HD = 128
SGU_BLOCK = 128
CHUNK = 64
D_CONV = 256
CONV_W = 3
D_IN = 2048
D_FF = 3584
N_EXPERTS = 8
EPS = 1e-6

LANES = 128
POOL_HALO = 32
CONV_HALO = 8
TM_MIX = 512
TM_FFN = 1024
TF_FFN = 512
TM_ROUTE = 512
TM_EXP = 1024
TM_MOVE = 1024
TOP_K = 2
VMEM_LIMIT = 48 * 1024 * 1024
M_I1, M_I2, M_R1, M_R2, M_G1, M_G2 = 0, 1, 2, 3, 4, 5
META_ROWS = 8

C_A, C_U, C_V, C_GB, C_GC, C_XC = 0, 256, 768, 1280, 1536, 1792

F32 = jnp.float32
BF16 = jnp.bfloat16


def _rms(x, g):
    ms = jnp.mean(x * x, axis=-1, keepdims=True)
    return (x * lax.rsqrt(ms + EPS)) * g


def _mixer_kernel(x_ref, n1g_ref, win_ref, poolw_ref, pscale_ref, lng_ref, lnb_ref, sguw_ref,
                  sgub_ref, convw_ref, gg_ref, wout_ref, o_ref, p_scr, a_scr, s2_scr, s4_scr,
                  s8_scr, z_scr, y_scr):
    tiles_per_seq = SEQ // TM_MIX
    seq_tile = pl.program_id(0) % tiles_per_seq

    @pl.when(seq_tile == 0)
    def _():
        a_scr[0:POOL_HALO, :] = jnp.zeros((POOL_HALO, D_POOL), F32)
        z_scr[0:CONV_HALO, :] = jnp.zeros((CONV_HALO, D_CONV), F32)

    x = x_ref[...]
    hb = _rms(x, n1g_ref[...]).astype(BF16)

    p_scr[...] = jnp.dot(hb, win_ref[...], preferred_element_type=F32)

    def proj(c0, width):
        return p_scr[:, c0:c0 + width]

    a = proj(C_A, D_POOL)
    rows = POOL_HALO + TM_MIX
    a_scr[POOL_HALO:rows, :] = a
    levels = (a_scr, s2_scr, s4_scr, s8_scr)
    for k in range(1, len(POOL_WINDOWS)):
        w, lo = POOL_WINDOWS[k - 1] // 2, 8 * k
        prev = levels[k - 1]
        levels[k][lo:rows, :] = prev[lo:rows, :] + prev[lo - w:rows - w, :]
    w_last = POOL_WINDOWS[-1] // 2
    s_last = s8_scr[POOL_HALO:rows, :] + s8_scr[POOL_HALO - w_last:rows - w_last, :]
    lane = lax.broadcasted_iota(jnp.int32, (1, D_POOL), 1)
    group = lane // POOL_GC
    wsum = jnp.where(group == 0, s2_scr[POOL_HALO:rows, :],
                     jnp.where(group == 1, s4_scr[POOL_HALO:rows, :],
                               jnp.where(group == 2, s8_scr[POOL_HALO:rows, :], s_last)))
    wlane = jnp.where(group == 0, POOL_WINDOWS[0],
                      jnp.where(group == 1, POOL_WINDOWS[1],
                                jnp.where(group == 2, POOL_WINDOWS[2], POOL_WINDOWS[3])))
    seen = seq_tile * TM_MIX + lax.broadcasted_iota(jnp.int32, (TM_MIX, 1), 0) + 1
    inv_count = jnp.where(seen >= wlane, 1.0 / wlane.astype(F32), 1.0 / seen.astype(F32))
    d = wsum * inv_count - a
    y_a = jnp.dot(d.astype(BF16), poolw_ref[...], preferred_element_type=F32) * pscale_ref[...]
    a_scr[0:POOL_HALO, :] = a_scr[TM_MIX:rows, :]
    gg = gg_ref[...]
    y_scr[:, 0:D_POOL] = _rms(y_a, gg[:, 0:D_POOL]).astype(BF16)

    gb = proj(C_GB, D_CONV)
    z = proj(C_GC, D_CONV) * proj(C_XC, D_CONV)
    z_scr[CONV_HALO:CONV_HALO + TM_MIX, :] = z
    cw = convw_ref[...]
    zc = (cw[0:1, :] * z_scr[CONV_HALO - 2:CONV_HALO - 2 + TM_MIX, :]
          + cw[1:2, :] * z_scr[CONV_HALO - 1:CONV_HALO - 1 + TM_MIX, :]
          + cw[2:3, :] * z)
    y_c = gb * zc
    z_scr[0:CONV_HALO, :] = z_scr[TM_MIX:TM_MIX + CONV_HALO, :]
    y_scr[:, D_POOL + D_SGU:] = _rms(y_c, gg[:, D_POOL + D_SGU:]).astype(BF16)

    u = proj(C_U, D_SGU)
    v = proj(C_V, D_SGU)
    mu = jnp.mean(v, axis=-1, keepdims=True)
    vc = v - mu
    var = jnp.mean(vc * vc, axis=-1, keepdims=True)
    vn = ((vc * lax.rsqrt(var + EPS)) * lng_ref[...] + lnb_ref[...]).astype(BF16)
    ci = lax.broadcasted_iota(jnp.int32, (SGU_BLOCK, SGU_BLOCK), 0) // CHUNK
    cj = lax.broadcasted_iota(jnp.int32, (SGU_BLOCK, SGU_BLOCK), 1) // CHUNK
    mask = (ci >= cj).astype(F32)
    sgub = sgub_ref[...]
    head_cols = []
    for hd in range(SGU_HEADS):
        wm = (sguw_ref[hd] * mask).astype(BF16)
        bias = sgub[:, hd:hd + 1]
        blocks = []
        for blk in range(TM_MIX // SGU_BLOCK):
            vblk = vn[blk * SGU_BLOCK:(blk + 1) * SGU_BLOCK, hd * SGU_HD:(hd + 1) * SGU_HD]
            blocks.append(jnp.dot(wm, vblk, preferred_element_type=F32) + bias)
        head_cols.append(jnp.concatenate(blocks, axis=0))
    mixed = jnp.concatenate(head_cols, axis=1)
    y_b = u * mixed
    y_scr[:, D_POOL:D_POOL + D_SGU] = _rms(y_b, gg[:, D_POOL:D_POOL + D_SGU]).astype(BF16)

    o_ref[...] = x + jnp.dot(y_scr[...], wout_ref[...], preferred_element_type=F32)


def _token_mixer(x2d, n1g, win, poolw_bd, pscale, lng, lnb, sguw, sgub_t, convw, gg, wout):
    n_tok = x2d.shape[0]
    const = lambda *shape: pl.BlockSpec(shape, lambda i: (0,) * len(shape))
    return pl.pallas_call(
        _mixer_kernel,
        grid=(n_tok // TM_MIX,),
        in_specs=[
            pl.BlockSpec((TM_MIX, D_MODEL), lambda i: (i, 0)),
            const(1, D_MODEL),
            const(D_MODEL, D_IN),
            const(D_POOL, D_POOL),
            const(1, D_POOL),
            const(1, D_SGU),
            const(1, D_SGU),
            const(SGU_HEADS, SGU_BLOCK, SGU_BLOCK),
            const(SGU_BLOCK, SGU_HEADS),
            const(CONV_W, D_CONV),
            const(1, D_MODEL),
            const(D_MODEL, D_MODEL),
        ],
        out_specs=pl.BlockSpec((TM_MIX, D_MODEL), lambda i: (i, 0)),
        out_shape=jax.ShapeDtypeStruct((n_tok, D_MODEL), F32),
        scratch_shapes=[
            pltpu.VMEM((TM_MIX, D_IN), F32),
            pltpu.VMEM((POOL_HALO + TM_MIX, D_POOL), F32),
            pltpu.VMEM((POOL_HALO + TM_MIX, D_POOL), F32),
            pltpu.VMEM((POOL_HALO + TM_MIX, D_POOL), F32),
            pltpu.VMEM((POOL_HALO + TM_MIX, D_POOL), F32),
            pltpu.VMEM((CONV_HALO + TM_MIX, D_CONV), F32),
            pltpu.VMEM((TM_MIX, D_MODEL), BF16),
        ],
        compiler_params=pltpu.CompilerParams(
            dimension_semantics=("arbitrary",), vmem_limit_bytes=VMEM_LIMIT),
        name="token_mixer",
    )(x2d, n1g, win, poolw_bd, pscale, lng, lnb, sguw, sgub_t, convw, gg, wout)


def _swiglu_step(hb, wg_ref, wu_ref, wd_ref):
    g = jnp.dot(hb, wg_ref[...].astype(BF16), preferred_element_type=F32)
    u = jnp.dot(hb, wu_ref[...].astype(BF16), preferred_element_type=F32)
    act = (g * jax.nn.sigmoid(g)) * u
    return jnp.dot(act.astype(BF16), wd_ref[...].astype(BF16), preferred_element_type=F32)


def _ffn_kernel(x_ref, n2g_ref, wg_ref, wu_ref, wd_ref, o_ref, hb_scr):
    @pl.when(pl.program_id(1) == 0)
    def _():
        x = x_ref[...]
        hb_scr[...] = _rms(x, n2g_ref[...]).astype(BF16)
        o_ref[...] = x

    o_ref[...] += _swiglu_step(hb_scr[...], wg_ref, wu_ref, wd_ref)


def _dense_ffn(x2d, n2g, wg, wu, wd):
    n_tok = x2d.shape[0]
    n_i = n_tok // TM_FFN
    return pl.pallas_call(
        _ffn_kernel,
        grid=(n_i, D_FF // TF_FFN),
        in_specs=[
            pl.BlockSpec((TM_FFN, D_MODEL), lambda i, j: (i, 0)),
            pl.BlockSpec((1, D_MODEL), lambda i, j: (0, 0)),
            pl.BlockSpec((D_MODEL, TF_FFN), lambda i, j: (0, j)),
            pl.BlockSpec((D_MODEL, TF_FFN), lambda i, j: (0, j)),
            pl.BlockSpec((TF_FFN, D_MODEL), lambda i, j: (j, 0)),
        ],
        out_specs=pl.BlockSpec((TM_FFN, D_MODEL), lambda i, j: (i, 0)),
        out_shape=jax.ShapeDtypeStruct((n_tok, D_MODEL), F32),
        scratch_shapes=[pltpu.VMEM((TM_FFN, D_MODEL), BF16)],
        compiler_params=pltpu.CompilerParams(
            dimension_semantics=("arbitrary", "arbitrary"), vmem_limit_bytes=VMEM_LIMIT),
        name="dense_ffn",
    )(x2d, n2g, wg, wu, wd)


def _router_kernel(x_ref, n2g_ref, rw_ref, rb_ref, meta_ref, meta_t_ref, cnt_ref):
    @pl.when(pl.program_id(0) == 0)
    def _():
        cnt_ref[...] = jnp.zeros_like(cnt_ref)

    hb = _rms(x_ref[...], n2g_ref[...]).astype(BF16)
    logits = jnp.dot(hb, rw_ref[...], preferred_element_type=F32) + rb_ref[...]
    lane = lax.broadcasted_iota(jnp.int32, logits.shape, 1).astype(F32)
    m1 = jnp.max(logits, axis=-1, keepdims=True)
    i1 = jnp.min(jnp.where(logits == m1, lane, float(LANES)), axis=-1, keepdims=True)
    rest = jnp.where(lane == i1, -jnp.inf, logits)
    m2 = jnp.max(rest, axis=-1, keepdims=True)
    i2 = jnp.min(jnp.where(rest == m2, lane, float(LANES)), axis=-1, keepdims=True)
    e2 = jnp.exp(m2 - m1)
    denom = 1.0 + e2

    sel = jnp.where((lane == i1) | (lane == i2), 1.0, 0.0)
    r = lax.broadcasted_iota(jnp.int32, (TM_ROUTE, TM_ROUTE), 0)
    c = lax.broadcasted_iota(jnp.int32, (TM_ROUTE, TM_ROUTE), 1)
    earlier = jnp.where(c < r, 1.0, 0.0).astype(BF16)
    before = jnp.dot(earlier, sel.astype(BF16), preferred_element_type=F32) + cnt_ref[...]
    rank1 = jnp.sum(jnp.where(lane == i1, before, 0.0), axis=-1, keepdims=True)
    rank2 = jnp.sum(jnp.where(lane == i2, before, 0.0), axis=-1, keepdims=True)
    cnt_ref[...] += jnp.sum(sel, axis=0, keepdims=True)

    record = jnp.zeros_like(logits)
    for k, val in ((M_I1, i1), (M_I2, i2), (M_R1, rank1), (M_R2, rank2),
                   (M_G1, 1.0 / denom), (M_G2, e2 / denom)):
        record = jnp.where(lane == k, val, record)
    meta_ref[...] = record
    meta_t_ref[...] = record.T[0:META_ROWS, :]


def _router(x2d, n2g, rw_pad, rb_pad):
    n_tok = x2d.shape[0]
    return pl.pallas_call(
        _router_kernel,
        grid=(n_tok // TM_ROUTE,),
        in_specs=[
            pl.BlockSpec((TM_ROUTE, D_MODEL), lambda i: (i, 0)),
            pl.BlockSpec((1, D_MODEL), lambda i: (0, 0)),
            pl.BlockSpec((D_MODEL, LANES), lambda i: (0, 0)),
            pl.BlockSpec((1, LANES), lambda i: (0, 0)),
        ],
        out_specs=[
            pl.BlockSpec((TM_ROUTE, LANES), lambda i: (i, 0)),
            pl.BlockSpec((META_ROWS, TM_ROUTE), lambda i: (0, i)),
            pl.BlockSpec((1, LANES), lambda i: (0, 0)),
        ],
        out_shape=[
            jax.ShapeDtypeStruct((n_tok, LANES), F32),
            jax.ShapeDtypeStruct((META_ROWS, n_tok), F32),
            jax.ShapeDtypeStruct((1, LANES), F32),
        ],
        compiler_params=pltpu.CompilerParams(
            dimension_semantics=("arbitrary",), vmem_limit_bytes=VMEM_LIMIT),
        name="router",
    )(x2d, n2g, rw_pad, rb_pad)


def _row_copy(src_ref, src_row, dst_ref, dst_row, sem):
    return pltpu.make_async_copy(src_ref.at[pl.ds(src_row, 1)], dst_ref.at[pl.ds(dst_row, 1)], sem)


def _dispatch_kernel(zstart_ref, zflag_ref, pos1_ref, pos2_ref, x_ref, xs_ref, zero_scr, sem):
    pos_refs = (pos1_ref, pos2_ref)
    step = pl.program_id(0)

    def wait_step_copies():
        for k in range(TOP_K):
            pltpu.make_async_copy(x_ref.at[pl.ds(0, TM_MOVE)], xs_ref.at[pl.ds(0, TM_MOVE)],
                                  sem.at[k]).wait()

    @pl.when(step == 0)
    def _():
        zero_scr[...] = jnp.zeros_like(zero_scr)
        for e in range(2 * N_EXPERTS):
            @pl.when(zflag_ref[e] > 0)
            def _():
                start = pl.multiple_of(zstart_ref[e], TM_EXP)
                fill = pltpu.make_async_copy(zero_scr, xs_ref.at[pl.ds(start, TM_EXP)], sem.at[0])
                fill.start()
                fill.wait()

    pl.when(step > 0)(wait_step_copies)

    def issue(r, carry):
        for k in range(TOP_K):
            _row_copy(x_ref, step * TM_MOVE + r, xs_ref, pos_refs[k][r], sem.at[k]).start()
        return carry

    lax.fori_loop(0, TM_MOVE, issue, 0, unroll=8)
    pl.when(step == pl.num_programs(0) - 1)(wait_step_copies)


def _dispatch(x2d, pos, zstart, zflag, n_rows):
    n_tok = x2d.shape[0]
    grid_spec = pltpu.PrefetchScalarGridSpec(
        num_scalar_prefetch=2,
        grid=(n_tok // TM_MOVE,),
        in_specs=[
            pl.BlockSpec((TM_MOVE,), lambda i, zs, zf: (i,), memory_space=pltpu.SMEM),
            pl.BlockSpec((TM_MOVE,), lambda i, zs, zf: (i,), memory_space=pltpu.SMEM),
            pl.BlockSpec(memory_space=pl.ANY),
        ],
        out_specs=pl.BlockSpec(memory_space=pl.ANY),
        scratch_shapes=[pltpu.VMEM((TM_EXP, D_MODEL), F32), pltpu.SemaphoreType.DMA((TOP_K,))],
    )
    return pl.pallas_call(
        _dispatch_kernel,
        grid_spec=grid_spec,
        out_shape=jax.ShapeDtypeStruct((n_rows, D_MODEL), F32),
        compiler_params=pltpu.CompilerParams(
            dimension_semantics=("arbitrary",), vmem_limit_bytes=VMEM_LIMIT),
        name="dispatch",
    )(zstart, zflag, pos[0], pos[1], x2d)


def _expert_kernel(te_ref, tr_ref, nv_ref, xs_ref, n2g_ref, wg_ref, wu_ref, wd_ref, y_ref,
                   hb_scr):
    @pl.when(pl.program_id(1) == 0)
    def _():
        hb_scr[...] = _rms(xs_ref[...], n2g_ref[...]).astype(BF16)
        y_ref[...] = jnp.zeros_like(y_ref)

    @pl.when(pl.program_id(0) < nv_ref[0])
    def _():
        y_ref[...] += _swiglu_step(hb_scr[...], wg_ref.at[0], wu_ref.at[0], wd_ref.at[0])


def _expert_ffn(xs, n2g, wg, wu, wd, tile_expert, tile_row, n_valid):
    n_tiles = xs.shape[0] // TM_EXP
    n_j = D_FF // TF_FFN

    def ff_block(i, j, nv):
        return jnp.where(i < nv[0], j, n_j - 1)

    grid_spec = pltpu.PrefetchScalarGridSpec(
        num_scalar_prefetch=3,
        grid=(n_tiles, n_j),
        in_specs=[
            pl.BlockSpec((TM_EXP, D_MODEL), lambda i, j, te, tr, nv: (tr[i], 0)),
            pl.BlockSpec((1, D_MODEL), lambda i, j, te, tr, nv: (0, 0)),
            pl.BlockSpec((1, D_MODEL, TF_FFN), lambda i, j, te, tr, nv: (te[i], 0, ff_block(i, j, nv))),
            pl.BlockSpec((1, D_MODEL, TF_FFN), lambda i, j, te, tr, nv: (te[i], 0, ff_block(i, j, nv))),
            pl.BlockSpec((1, TF_FFN, D_MODEL), lambda i, j, te, tr, nv: (te[i], ff_block(i, j, nv), 0)),
        ],
        out_specs=pl.BlockSpec((TM_EXP, D_MODEL), lambda i, j, te, tr, nv: (i, 0)),
        scratch_shapes=[pltpu.VMEM((TM_EXP, D_MODEL), BF16)],
    )
    return pl.pallas_call(
        _expert_kernel,
        grid_spec=grid_spec,
        out_shape=jax.ShapeDtypeStruct(xs.shape, F32),
        compiler_params=pltpu.CompilerParams(
            dimension_semantics=("arbitrary", "arbitrary"), vmem_limit_bytes=VMEM_LIMIT),
        name="expert_ffn",
    )(tile_expert, tile_row, n_valid, xs, n2g, wg, wu, wd)


def _combine_kernel(pos1_ref, pos2_ref, pos1_next_ref, pos2_next_ref, x_ref, meta_ref, fg_ref,
                    y_ref, o_ref, y_scr, sem):
    step = pl.program_id(0)
    cur = step % 2

    def gather(pos_refs, b):
        def issue(r, carry):
            for k in range(TOP_K):
                _row_copy(y_ref, pos_refs[k][r], y_scr.at[b, k], r, sem.at[b, k]).start()
            return carry

        lax.fori_loop(0, TM_MOVE, issue, 0, unroll=8)

    @pl.when(step == 0)
    def _():
        gather((pos1_ref, pos2_ref), 0)

    @pl.when(step + 1 < pl.num_programs(0))
    def _():
        gather((pos1_next_ref, pos2_next_ref), 1 - cur)

    for k in range(TOP_K):
        pltpu.make_async_copy(y_ref.at[pl.ds(0, TM_MOVE)], y_scr.at[cur, k], sem.at[cur, k]).wait()

    meta = meta_ref[...]
    moe = meta[:, M_G1:M_G1 + 1] * y_scr[cur, 0] + meta[:, M_G2:M_G2 + 1] * y_scr[cur, 1]
    o_ref[...] = _rms(x_ref[...] + moe, fg_ref[...])


def _combine(x2d, meta, fg, y, pos):
    n_tok = x2d.shape[0]
    n_steps = n_tok // TM_MOVE
    next_block = lambda i: (jnp.minimum(i + 1, n_steps - 1),)
    return pl.pallas_call(
        _combine_kernel,
        grid=(n_steps,),
        in_specs=[
            pl.BlockSpec((TM_MOVE,), lambda i: (i,), memory_space=pltpu.SMEM),
            pl.BlockSpec((TM_MOVE,), lambda i: (i,), memory_space=pltpu.SMEM),
            pl.BlockSpec((TM_MOVE,), next_block, memory_space=pltpu.SMEM),
            pl.BlockSpec((TM_MOVE,), next_block, memory_space=pltpu.SMEM),
            pl.BlockSpec((TM_MOVE, D_MODEL), lambda i: (i, 0)),
            pl.BlockSpec((TM_MOVE, LANES), lambda i: (i, 0)),
            pl.BlockSpec((1, D_MODEL), lambda i: (0, 0)),
            pl.BlockSpec(memory_space=pl.ANY),
        ],
        out_specs=pl.BlockSpec((TM_MOVE, D_MODEL), lambda i: (i, 0)),
        out_shape=jax.ShapeDtypeStruct((n_tok, D_MODEL), F32),
        scratch_shapes=[pltpu.VMEM((2, TOP_K, TM_MOVE, D_MODEL), F32),
                        pltpu.SemaphoreType.DMA((2, TOP_K))],
        compiler_params=pltpu.CompilerParams(
            dimension_semantics=("arbitrary",), vmem_limit_bytes=VMEM_LIMIT),
        name="combine",
    )(pos[0], pos[1], pos[0], pos[1], x2d, meta, fg, y)


def _routing_tables(meta_t, counts_f, n_tiles):
    counts = counts_f[0, :N_EXPERTS].astype(jnp.int32)
    padded = ((counts + TM_EXP - 1) // TM_EXP) * TM_EXP
    ends = jnp.cumsum(padded)
    starts = ends - padded
    idx = meta_t[M_I1:M_I2 + 1].astype(jnp.int32)
    pos = meta_t[M_R1:M_R2 + 1].astype(jnp.int32)
    for e in range(N_EXPERTS):
        pos = pos + jnp.where(idx == e, starts[e], 0)
    n_valid = ends[-1] // TM_EXP
    tile = jnp.minimum(jnp.arange(n_tiles, dtype=jnp.int32), n_valid - 1)
    tile_expert = jnp.sum(ends[None, :] <= (tile * TM_EXP)[:, None], axis=-1).astype(jnp.int32)
    tile_expert = jnp.minimum(tile_expert, N_EXPERTS - 1)
    tail = jnp.arange(n_tiles - N_EXPERTS, n_tiles, dtype=jnp.int32)
    zstart = jnp.concatenate([ends - TM_EXP, tail * TM_EXP]).astype(jnp.int32)
    zflag = jnp.concatenate([padded > 0, tail >= n_valid]).astype(jnp.int32)
    return (pos.astype(jnp.int32), tile_expert, tile.astype(jnp.int32),
            n_valid.reshape(1).astype(jnp.int32), zstart, zflag)


def _block_diag(blocks):
    n, r, c = blocks.shape
    eye = jnp.eye(n, dtype=blocks.dtype)
    return (eye[:, None, :, None] * blocks[:, :, None, :]).reshape(n * r, n * c)


def kernel(x, norm1_g, w_in, pool_w, pool_scale, sgu_ln_g, sgu_ln_b, sgu_w, sgu_b, conv_w,
           group_g, w_out, norm2_g, ffn_w_gate, ffn_w_up, ffn_w_down, router_w, router_b,
           moe_w_gate, moe_w_up, moe_w_down, final_g):
    bsz, seq, d = x.shape
    assert (seq, d) == (SEQ, D_MODEL) and DEPTH == 2
    x2d = x.reshape(bsz * seq, d)
    row = lambda t: t.reshape(1, -1)

    def mixer(l, xin):
        return _token_mixer(
            xin, row(norm1_g[l]), w_in[l].astype(BF16), _block_diag(pool_w[l]).astype(BF16),
            row(pool_scale[l]), row(sgu_ln_g[l]), row(sgu_ln_b[l]), sgu_w[l], sgu_b[l].T,
            conv_w[l], row(group_g[l]), w_out[l].astype(BF16))

    x2d = mixer(0, x2d)
    x2d = _dense_ffn(x2d, row(norm2_g[0]), ffn_w_gate[0], ffn_w_up[0], ffn_w_down[0])
    x2d = mixer(1, x2d)
    rw_pad = jnp.zeros((d, LANES), BF16).at[:, :N_EXPERTS].set(router_w[0].astype(BF16))
    rb_pad = jnp.full((1, LANES), -1e30, F32).at[0, :N_EXPERTS].set(router_b[0])
    n2g = row(norm2_g[1])
    meta, meta_t, counts = _router(x2d, n2g, rw_pad, rb_pad)
    n_tiles = (TOP_K * bsz * seq) // TM_EXP + N_EXPERTS
    pos, tile_expert, tile_row, n_valid, zstart, zflag = _routing_tables(meta_t, counts, n_tiles)
    xs = _dispatch(x2d, pos, zstart, zflag, n_tiles * TM_EXP)
    y = _expert_ffn(xs, n2g, moe_w_gate[0], moe_w_up[0], moe_w_down[0], tile_expert, tile_row, n_valid)
    out = _combine(x2d, meta, row(final_g), y, pos)
    return out.reshape(bsz, seq, d)
```

```python
import functools

import jax
import jax.numpy as jnp
from jax import lax
from jax.experimental import pallas as pl
from jax.experimental.pallas import tpu as pltpu

D_MODEL = 1024
SEQ = 2048
DEPTH = 2
POOL_WINDOWS = (2, 4, 8, 16)
POOL_GC = 64
D_POOL = 256
D_SGU = 512
SGU_HEADS = 4
SGU_HD = 128
SGU_BLOCK = 128
CHUNK = 64
D_CONV = 256
CONV_W = 3
D_IN = 2048
D_FF = 3584
N_EXPERTS = 8
EPS = 1e-6

LANES = 128
POOL_HALO = 32
CONV_HALO = 8
TM_MIX = 512
TM_FFN = 1024
TF_FFN = 512
TM_ROUTE = 512
TM_EXP = 1024
TM_MOVE = 1024
TOP_K = 2
VMEM_LIMIT = 48 * 1024 * 1024
M_I1, M_I2, M_R1, M_R2, M_G1, M_G2 = 0, 1, 2, 3, 4, 5
META_ROWS = 8

C_A, C_U, C_V, C_GB, C_GC, C_XC = 0, 256, 768, 1280, 1536, 1792

F32 = jnp.float32
BF16 = jnp.bfloat16


def _layer_row(ref, layer):
    return ref[layer:layer + 1, :]


def _rms(x, g):
    ms = jnp.mean(x * x, axis=-1, keepdims=True)
    return (x * lax.rsqrt(ms + EPS)) * g


def _mixer_kernel(layer, x_ref, n1g_ref, win_ref, poolw_ref, pscale_ref, lng_ref, lnb_ref, sguw_ref,
                  sgub_ref, convw_ref, gg_ref, wout_ref, o_ref, p_scr, a_scr, s2_scr, s4_scr,
                  s8_scr, z_scr, y_scr):
    tiles_per_seq = SEQ // TM_MIX
    seq_tile = pl.program_id(0) % tiles_per_seq

    @pl.when(seq_tile == 0)
    def _():
        a_scr[0:POOL_HALO, :] = jnp.zeros((POOL_HALO, D_POOL), F32)
        z_scr[0:CONV_HALO, :] = jnp.zeros((CONV_HALO, D_CONV), F32)

    x = x_ref[...]
    hb = _rms(x, _layer_row(n1g_ref, layer)).astype(BF16)

    p_scr[...] = jnp.dot(hb, win_ref[...], preferred_element_type=F32)

    def proj(c0, width):
        return p_scr[:, c0:c0 + width]

    a = proj(C_A, D_POOL)
    rows = POOL_HALO + TM_MIX
    a_scr[POOL_HALO:rows, :] = a
    levels = (a_scr, s2_scr, s4_scr, s8_scr)
    for k in range(1, len(POOL_WINDOWS)):
        w, lo = POOL_WINDOWS[k - 1] // 2, 8 * k
        prev = levels[k - 1]
        levels[k][lo:rows, :] = prev[lo:rows, :] + prev[lo - w:rows - w, :]
    w_last = POOL_WINDOWS[-1] // 2
    s_last = s8_scr[POOL_HALO:rows, :] + s8_scr[POOL_HALO - w_last:rows - w_last, :]
    lane = lax.broadcasted_iota(jnp.int32, (1, D_POOL), 1)
    group = lane // POOL_GC
    wsum = jnp.where(group == 0, s2_scr[POOL_HALO:rows, :],
                     jnp.where(group == 1, s4_scr[POOL_HALO:rows, :],
                               jnp.where(group == 2, s8_scr[POOL_HALO:rows, :], s_last)))
    wlane = jnp.where(group == 0, POOL_WINDOWS[0],
                      jnp.where(group == 1, POOL_WINDOWS[1],
                                jnp.where(group == 2, POOL_WINDOWS[2], POOL_WINDOWS[3])))
    seen = seq_tile * TM_MIX + lax.broadcasted_iota(jnp.int32, (TM_MIX, 1), 0) + 1
    inv_count = jnp.where(seen >= wlane, 1.0 / wlane.astype(F32), 1.0 / seen.astype(F32))
    d = wsum * inv_count - a
    y_a = jnp.dot(d.astype(BF16), poolw_ref[...], preferred_element_type=F32)
    y_a = y_a * _layer_row(pscale_ref, layer)
    a_scr[0:POOL_HALO, :] = a_scr[TM_MIX:rows, :]
    gg = _layer_row(gg_ref, layer)
    y_scr[:, 0:D_POOL] = _rms(y_a, gg[:, 0:D_POOL]).astype(BF16)

    gb = proj(C_GB, D_CONV)
    z = proj(C_GC, D_CONV) * proj(C_XC, D_CONV)
    z_scr[CONV_HALO:CONV_HALO + TM_MIX, :] = z
    cw = convw_ref[...]
    zc = (cw[0:1, :] * z_scr[CONV_HALO - 2:CONV_HALO - 2 + TM_MIX, :]
          + cw[1:2, :] * z_scr[CONV_HALO - 1:CONV_HALO - 1 + TM_MIX, :]
          + cw[2:3, :] * z)
    y_c = gb * zc
    z_scr[0:CONV_HALO, :] = z_scr[TM_MIX:TM_MIX + CONV_HALO, :]
    y_scr[:, D_POOL + D_SGU:] = _rms(y_c, gg[:, D_POOL + D_SGU:]).astype(BF16)

    u = proj(C_U, D_SGU)
    v = proj(C_V, D_SGU)
    mu = jnp.mean(v, axis=-1, keepdims=True)
    vc = v - mu
    var = jnp.mean(vc * vc, axis=-1, keepdims=True)
    vn = ((vc * lax.rsqrt(var + EPS)) * _layer_row(lng_ref, layer)
          + _layer_row(lnb_ref, layer)).astype(BF16)
    ci = lax.broadcasted_iota(jnp.int32, (SGU_BLOCK, SGU_BLOCK), 0) // CHUNK
    cj = lax.broadcasted_iota(jnp.int32, (SGU_BLOCK, SGU_BLOCK), 1) // CHUNK
    mask = (ci >= cj).astype(F32)
    sgub = sgub_ref[...]
    head_cols = []
    for hd in range(SGU_HEADS):
        wm = (sguw_ref[hd] * mask).astype(BF16)
        bias = sgub[:, hd:hd + 1]
        blocks = []
        for blk in range(TM_MIX // SGU_BLOCK):
            vblk = vn[blk * SGU_BLOCK:(blk + 1) * SGU_BLOCK, hd * SGU_HD:(hd + 1) * SGU_HD]
            blocks.append(jnp.dot(wm, vblk, preferred_element_type=F32) + bias)
        head_cols.append(jnp.concatenate(blocks, axis=0))
    mixed = jnp.concatenate(head_cols, axis=1)
    y_b = u * mixed
    y_scr[:, D_POOL:D_POOL + D_SGU] = _rms(y_b, gg[:, D_POOL:D_POOL + D_SGU]).astype(BF16)

    o_ref[...] = x + jnp.dot(y_scr[...], wout_ref[...], preferred_element_type=F32)


def _token_mixer(layer, x2d, n1g, win, poolw_bd, pscale, lng, lnb, sguw, sgub_t, convw, gg, wout):
    n_tok = x2d.shape[0]
    vec = lambda n: pl.BlockSpec((DEPTH, n), lambda i: (0, 0))
    mat = lambda *shape: pl.BlockSpec((None,) + shape, lambda i: (layer,) + (0,) * len(shape))
    return pl.pallas_call(
        functools.partial(_mixer_kernel, layer),
        grid=(n_tok // TM_MIX,),
        in_specs=[
            pl.BlockSpec((TM_MIX, D_MODEL), lambda i: (i, 0)),
            vec(D_MODEL),
            mat(D_MODEL, D_IN),
            mat(D_POOL, D_POOL),
            vec(D_POOL),
            vec(D_SGU),
            vec(D_SGU),
            mat(SGU_HEADS, SGU_BLOCK, SGU_BLOCK),
            mat(SGU_BLOCK, SGU_HEADS),
            mat(CONV_W, D_CONV),
            vec(D_MODEL),
            mat(D_MODEL, D_MODEL),
        ],
        out_specs=pl.BlockSpec((TM_MIX, D_MODEL), lambda i: (i, 0)),
        out_shape=jax.ShapeDtypeStruct((n_tok, D_MODEL), F32),
        scratch_shapes=[
            pltpu.VMEM((TM_MIX, D_IN), F32),
            pltpu.VMEM((POOL_HALO + TM_MIX, D_POOL), F32),
            pltpu.VMEM((POOL_HALO + TM_MIX, D_POOL), F32),
            pltpu.VMEM((POOL_HALO + TM_MIX, D_POOL), F32),
            pltpu.VMEM((POOL_HALO + TM_MIX, D_POOL), F32),
            pltpu.VMEM((CONV_HALO + TM_MIX, D_CONV), F32),
            pltpu.VMEM((TM_MIX, D_MODEL), BF16),
        ],
        compiler_params=pltpu.CompilerParams(
            dimension_semantics=("arbitrary",), vmem_limit_bytes=VMEM_LIMIT),
        name="token_mixer",
    )(x2d, n1g, win, poolw_bd, pscale, lng, lnb, sguw, sgub_t, convw, gg, wout)


def _swiglu_step(hb, wg_ref, wu_ref, wd_ref):
    g = jnp.dot(hb, wg_ref[...].astype(BF16), preferred_element_type=F32)
    u = jnp.dot(hb, wu_ref[...].astype(BF16), preferred_element_type=F32)
    act = (g * jax.nn.sigmoid(g)) * u
    return jnp.dot(act.astype(BF16), wd_ref[...].astype(BF16), preferred_element_type=F32)


def _ffn_kernel(layer, x_ref, n2g_ref, wg_ref, wu_ref, wd_ref, o_ref, hb_scr):
    @pl.when(pl.program_id(1) == 0)
    def _():
        x = x_ref[...]
        hb_scr[...] = _rms(x, _layer_row(n2g_ref, layer)).astype(BF16)
        o_ref[...] = x

    o_ref[...] += _swiglu_step(hb_scr[...], wg_ref, wu_ref, wd_ref)


def _dense_ffn(layer, x2d, n2g, wg, wu, wd):
    n_tok = x2d.shape[0]
    return pl.pallas_call(
        functools.partial(_ffn_kernel, layer),
        grid=(n_tok // TM_FFN, D_FF // TF_FFN),
        in_specs=[
            pl.BlockSpec((TM_FFN, D_MODEL), lambda i, j: (i, 0)),
            pl.BlockSpec((DEPTH, D_MODEL), lambda i, j: (0, 0)),
            pl.BlockSpec((None, D_MODEL, TF_FFN), lambda i, j: (layer // 2, 0, j)),
            pl.BlockSpec((None, D_MODEL, TF_FFN), lambda i, j: (layer // 2, 0, j)),
            pl.BlockSpec((None, TF_FFN, D_MODEL), lambda i, j: (layer // 2, j, 0)),
        ],
        out_specs=pl.BlockSpec((TM_FFN, D_MODEL), lambda i, j: (i, 0)),
        out_shape=jax.ShapeDtypeStruct((n_tok, D_MODEL), F32),
        scratch_shapes=[pltpu.VMEM((TM_FFN, D_MODEL), BF16)],
        compiler_params=pltpu.CompilerParams(
            dimension_semantics=("arbitrary", "arbitrary"), vmem_limit_bytes=VMEM_LIMIT),
        name="dense_ffn",
    )(x2d, n2g, wg, wu, wd)


def _router_kernel(layer, x_ref, n2g_ref, rw_ref, rb_ref, meta_ref, meta_t_ref, cnt_ref):
    @pl.when(pl.program_id(0) == 0)
    def _():
        cnt_ref[...] = jnp.zeros_like(cnt_ref)

    hb = _rms(x_ref[...], _layer_row(n2g_ref, layer)).astype(BF16)
    logits = jnp.dot(hb, rw_ref[...], preferred_element_type=F32) + rb_ref[...]
    lane = lax.broadcasted_iota(jnp.int32, logits.shape, 1).astype(F32)
    m1 = jnp.max(logits, axis=-1, keepdims=True)
    i1 = jnp.min(jnp.where(logits == m1, lane, float(LANES)), axis=-1, keepdims=True)
    rest = jnp.where(lane == i1, -jnp.inf, logits)
    m2 = jnp.max(rest, axis=-1, keepdims=True)
    i2 = jnp.min(jnp.where(rest == m2, lane, float(LANES)), axis=-1, keepdims=True)
    e2 = jnp.exp(m2 - m1)
    denom = 1.0 + e2

    sel = jnp.where((lane == i1) | (lane == i2), 1.0, 0.0)
    r = lax.broadcasted_iota(jnp.int32, (TM_ROUTE, TM_ROUTE), 0)
    c = lax.broadcasted_iota(jnp.int32, (TM_ROUTE, TM_ROUTE), 1)
    earlier = jnp.where(c < r, 1.0, 0.0).astype(BF16)
    before = jnp.dot(earlier, sel.astype(BF16), preferred_element_type=F32) + cnt_ref[...]
    rank1 = jnp.sum(jnp.where(lane == i1, before, 0.0), axis=-1, keepdims=True)
    rank2 = jnp.sum(jnp.where(lane == i2, before, 0.0), axis=-1, keepdims=True)
    cnt_ref[...] += jnp.sum(sel, axis=0, keepdims=True)

    record = jnp.zeros_like(logits)
    for k, val in ((M_I1, i1), (M_I2, i2), (M_R1, rank1), (M_R2, rank2),
                   (M_G1, 1.0 / denom), (M_G2, e2 / denom)):
        record = jnp.where(lane == k, val, record)
    meta_ref[...] = record
    meta_t_ref[...] = record.T[0:META_ROWS, :]


def _router(layer, x2d, n2g, rw_pad, rb_pad):
    n_tok = x2d.shape[0]
    return pl.pallas_call(
        functools.partial(_router_kernel, layer),
        grid=(n_tok // TM_ROUTE,),
        in_specs=[
            pl.BlockSpec((TM_ROUTE, D_MODEL), lambda i: (i, 0)),
            pl.BlockSpec((DEPTH, D_MODEL), lambda i: (0, 0)),
            pl.BlockSpec((D_MODEL, LANES), lambda i: (0, 0)),
            pl.BlockSpec((1, LANES), lambda i: (0, 0)),
        ],
        out_specs=[
            pl.BlockSpec((TM_ROUTE, LANES), lambda i: (i, 0)),
            pl.BlockSpec((META_ROWS, TM_ROUTE), lambda i: (0, i)),
            pl.BlockSpec((1, LANES), lambda i: (0, 0)),
        ],
        out_shape=[
            jax.ShapeDtypeStruct((n_tok, LANES), F32),
            jax.ShapeDtypeStruct((META_ROWS, n_tok), F32),
            jax.ShapeDtypeStruct((1, LANES), F32),
        ],
        compiler_params=pltpu.CompilerParams(
            dimension_semantics=("arbitrary",), vmem_limit_bytes=VMEM_LIMIT),
        name="router",
    )(x2d, n2g, rw_pad, rb_pad)


def _row_copy(src_ref, src_row, dst_ref, dst_row, sem):
    return pltpu.make_async_copy(src_ref.at[pl.ds(src_row, 1)], dst_ref.at[pl.ds(dst_row, 1)], sem)


def _dispatch_kernel(zstart_ref, zflag_ref, pos1_ref, pos2_ref, x_ref, xs_ref, x_scr, zero_scr,
                     sem_in, sem_rows, sem_fill):
    pos_refs = (pos1_ref, pos2_ref)
    step = pl.program_id(0)
    cur = step % 2

    def tile_load(tile, b):
        start = pl.multiple_of(tile * TM_MOVE, TM_MOVE)
        return pltpu.make_async_copy(x_ref.at[pl.ds(start, TM_MOVE)], x_scr.at[b], sem_in.at[b])

    def wait_rows(b):
        for k in range(TOP_K):
            pltpu.make_async_copy(x_scr.at[b], xs_ref.at[pl.ds(0, TM_MOVE)], sem_rows.at[b, k]).wait()

    @pl.when(step == 0)
    def _():
        tile_load(0, 0).start()
        zero_scr[...] = jnp.zeros_like(zero_scr)
        for e in range(2 * N_EXPERTS):
            @pl.when(zflag_ref[e] > 0)
            def _():
                start = pl.multiple_of(zstart_ref[e], TM_EXP)
                fill = pltpu.make_async_copy(zero_scr, xs_ref.at[pl.ds(start, TM_EXP)], sem_fill)
                fill.start()
                fill.wait()

    @pl.when(step > 0)
    def _():
        wait_rows(1 - cur)

    @pl.when(step + 1 < pl.num_programs(0))
    def _():
        tile_load(step + 1, 1 - cur).start()

    tile_load(step, cur).wait()

    def issue(r, carry):
        for k in range(TOP_K):
            _row_copy(x_scr.at[cur], r, xs_ref, pos_refs[k][r], sem_rows.at[cur, k]).start()
        return carry

    lax.fori_loop(0, TM_MOVE, issue, 0, unroll=8)

    @pl.when(step == pl.num_programs(0) - 1)
    def _():
        wait_rows(cur)


def _dispatch(x2d, pos, zstart, zflag, n_rows):
    n_tok = x2d.shape[0]
    grid_spec = pltpu.PrefetchScalarGridSpec(
        num_scalar_prefetch=2,
        grid=(n_tok // TM_MOVE,),
        in_specs=[
            pl.BlockSpec((TM_MOVE,), lambda i, zs, zf: (i,), memory_space=pltpu.SMEM),
            pl.BlockSpec((TM_MOVE,), lambda i, zs, zf: (i,), memory_space=pltpu.SMEM),
            pl.BlockSpec(memory_space=pl.ANY),
        ],
        out_specs=pl.BlockSpec(memory_space=pl.ANY),
        scratch_shapes=[pltpu.VMEM((2, TM_MOVE, D_MODEL), F32), pltpu.VMEM((TM_EXP, D_MODEL), F32),
                        pltpu.SemaphoreType.DMA((2,)), pltpu.SemaphoreType.DMA((2, TOP_K)),
                        pltpu.SemaphoreType.DMA(())],
    )
    return pl.pallas_call(
        _dispatch_kernel,
        grid_spec=grid_spec,
        out_shape=jax.ShapeDtypeStruct((n_rows, D_MODEL), F32),
        compiler_params=pltpu.CompilerParams(
            dimension_semantics=("arbitrary",), vmem_limit_bytes=VMEM_LIMIT),
        name="dispatch",
    )(zstart, zflag, pos[0], pos[1], x2d)


def _expert_kernel(layer, te_ref, tr_ref, nv_ref, xs_ref, n2g_ref, wg_ref, wu_ref, wd_ref, y_ref,
                   hb_scr):
    @pl.when(pl.program_id(1) == 0)
    def _():
        hb_scr[...] = _rms(xs_ref[...], _layer_row(n2g_ref, layer)).astype(BF16)
        y_ref[...] = jnp.zeros_like(y_ref)

    @pl.when(pl.program_id(0) < nv_ref[0])
    def _():
        y_ref[...] += _swiglu_step(hb_scr[...], wg_ref.at[0], wu_ref.at[0], wd_ref.at[0])


def _expert_ffn(layer, xs, n2g, wg, wu, wd, tile_expert, tile_row, n_valid):
    n_tiles = xs.shape[0] // TM_EXP
    n_j = D_FF // TF_FFN
    moe = layer // 2

    def ff_block(i, j, nv):
        return jnp.where(i < nv[0], j, n_j - 1)

    grid_spec = pltpu.PrefetchScalarGridSpec(
        num_scalar_prefetch=3,
        grid=(n_tiles, n_j),
        in_specs=[
            pl.BlockSpec((TM_EXP, D_MODEL), lambda i, j, te, tr, nv: (tr[i], 0)),
            pl.BlockSpec((DEPTH, D_MODEL), lambda i, j, te, tr, nv: (0, 0)),
            pl.BlockSpec((None, 1, D_MODEL, TF_FFN),
                         lambda i, j, te, tr, nv: (moe, te[i], 0, ff_block(i, j, nv))),
            pl.BlockSpec((None, 1, D_MODEL, TF_FFN),
                         lambda i, j, te, tr, nv: (moe, te[i], 0, ff_block(i, j, nv))),
            pl.BlockSpec((None, 1, TF_FFN, D_MODEL),
                         lambda i, j, te, tr, nv: (moe, te[i], ff_block(i, j, nv), 0)),
        ],
        out_specs=pl.BlockSpec((TM_EXP, D_MODEL), lambda i, j, te, tr, nv: (i, 0)),
        scratch_shapes=[pltpu.VMEM((TM_EXP, D_MODEL), BF16)],
    )
    return pl.pallas_call(
        functools.partial(_expert_kernel, layer),
        grid_spec=grid_spec,
        out_shape=jax.ShapeDtypeStruct(xs.shape, F32),
        compiler_params=pltpu.CompilerParams(
            dimension_semantics=("arbitrary", "arbitrary"), vmem_limit_bytes=VMEM_LIMIT),
        name="expert_ffn",
    )(tile_expert, tile_row, n_valid, xs, n2g, wg, wu, wd)


def _combine_kernel(pos1_ref, pos2_ref, pos1_next_ref, pos2_next_ref, x_ref, meta_ref, fg_ref,
                    y_ref, o_ref, y_scr, sem):
    step = pl.program_id(0)
    cur = step % 2

    def gather(pos_refs, b):
        def issue(r, carry):
            for k in range(TOP_K):
                _row_copy(y_ref, pos_refs[k][r], y_scr.at[b, k], r, sem.at[b, k]).start()
            return carry

        lax.fori_loop(0, TM_MOVE, issue, 0, unroll=8)

    @pl.when(step == 0)
    def _():
        gather((pos1_ref, pos2_ref), 0)

    @pl.when(step + 1 < pl.num_programs(0))
    def _():
        gather((pos1_next_ref, pos2_next_ref), 1 - cur)

    for k in range(TOP_K):
        pltpu.make_async_copy(y_ref.at[pl.ds(0, TM_MOVE)], y_scr.at[cur, k], sem.at[cur, k]).wait()

    meta = meta_ref[...]
    moe = meta[:, M_G1:M_G1 + 1] * y_scr[cur, 0] + meta[:, M_G2:M_G2 + 1] * y_scr[cur, 1]
    o_ref[...] = _rms(x_ref[...] + moe, fg_ref[...])


def _combine(x2d, meta, fg, y, pos):
    n_tok = x2d.shape[0]
    n_steps = n_tok // TM_MOVE
    next_block = lambda i: (jnp.minimum(i + 1, n_steps - 1),)
    return pl.pallas_call(
        _combine_kernel,
        grid=(n_steps,),
        in_specs=[
            pl.BlockSpec((TM_MOVE,), lambda i: (i,), memory_space=pltpu.SMEM),
            pl.BlockSpec((TM_MOVE,), lambda i: (i,), memory_space=pltpu.SMEM),
            pl.BlockSpec((TM_MOVE,), next_block, memory_space=pltpu.SMEM),
            pl.BlockSpec((TM_MOVE,), next_block, memory_space=pltpu.SMEM),
            pl.BlockSpec((TM_MOVE, D_MODEL), lambda i: (i, 0)),
            pl.BlockSpec((TM_MOVE, LANES), lambda i: (i, 0)),
            pl.BlockSpec((1, D_MODEL), lambda i: (0, 0)),
            pl.BlockSpec(memory_space=pl.ANY),
        ],
        out_specs=pl.BlockSpec((TM_MOVE, D_MODEL), lambda i: (i, 0)),
        out_shape=jax.ShapeDtypeStruct((n_tok, D_MODEL), F32),
        scratch_shapes=[pltpu.VMEM((2, TOP_K, TM_MOVE, D_MODEL), F32),
                        pltpu.SemaphoreType.DMA((2, TOP_K))],
        compiler_params=pltpu.CompilerParams(
            dimension_semantics=("arbitrary",), vmem_limit_bytes=VMEM_LIMIT),
        name="combine",
    )(pos[0], pos[1], pos[0], pos[1], x2d, meta, fg, y)


def _routing_tables(meta_t, counts_f, n_tiles):
    counts = counts_f[0, :N_EXPERTS].astype(jnp.int32)
    padded = ((counts + TM_EXP - 1) // TM_EXP) * TM_EXP
    ends = jnp.cumsum(padded)
    starts = ends - padded
    idx = meta_t[M_I1:M_I2 + 1].astype(jnp.int32)
    pos = meta_t[M_R1:M_R2 + 1].astype(jnp.int32)
    for e in range(N_EXPERTS):
        pos = pos + jnp.where(idx == e, starts[e], 0)
    n_valid = ends[-1] // TM_EXP
    tile = jnp.minimum(jnp.arange(n_tiles, dtype=jnp.int32), n_valid - 1)
    tile_expert = jnp.sum(ends[None, :] <= (tile * TM_EXP)[:, None], axis=-1).astype(jnp.int32)
    tile_expert = jnp.minimum(tile_expert, N_EXPERTS - 1)
    tail = jnp.arange(n_tiles - N_EXPERTS, n_tiles, dtype=jnp.int32)
    zstart = jnp.concatenate([ends - TM_EXP, tail * TM_EXP]).astype(jnp.int32)
    zflag = jnp.concatenate([padded > 0, tail >= n_valid]).astype(jnp.int32)
    return (pos.astype(jnp.int32), tile_expert, tile.astype(jnp.int32),
            n_valid.reshape(1).astype(jnp.int32), zstart, zflag)


def _block_diag(blocks):
    layers, n, r, c = blocks.shape
    eye = jnp.eye(n, dtype=blocks.dtype)
    return (eye[None, :, None, :, None] * blocks[:, :, :, None, :]).reshape(layers, n * r, n * c)


def kernel(x, norm1_g, w_in, pool_w, pool_scale, sgu_ln_g, sgu_ln_b, sgu_w, sgu_b, conv_w,
           group_g, w_out, norm2_g, ffn_w_gate, ffn_w_up, ffn_w_down, router_w, router_b,
           moe_w_gate, moe_w_up, moe_w_down, final_g):
    bsz, seq, d = x.shape
    assert (seq, d) == (SEQ, D_MODEL) and DEPTH == 2
    x2d = x.reshape(bsz * seq, d)
    mixer_params = (norm1_g, w_in.astype(BF16), _block_diag(pool_w).astype(BF16), pool_scale,
                    sgu_ln_g, sgu_ln_b, sgu_w, sgu_b.transpose(0, 2, 1), conv_w, group_g,
                    w_out.astype(BF16))

    x2d = _token_mixer(0, x2d, *mixer_params)
    x2d = _dense_ffn(0, x2d, norm2_g, ffn_w_gate, ffn_w_up, ffn_w_down)
    x2d = _token_mixer(1, x2d, *mixer_params)
    rw_pad = jnp.zeros((d, LANES), BF16).at[:, :N_EXPERTS].set(router_w[0].astype(BF16))
    rb_pad = jnp.full((1, LANES), -1e30, F32).at[0, :N_EXPERTS].set(router_b[0])
    meta, meta_t, counts = _router(1, x2d, norm2_g, rw_pad, rb_pad)
    n_tiles = (TOP_K * bsz * seq) // TM_EXP + N_EXPERTS
    pos, tile_expert, tile_row, n_valid, zstart, zflag = _routing_tables(meta_t, counts, n_tiles)
    xs = _dispatch(x2d, pos, zstart, zflag, n_tiles * TM_EXP)
    y = _expert_ffn(1, xs, norm2_g, moe_w_gate, moe_w_up, moe_w_down, tile_expert, tile_row, n_valid)
    out = _combine(x2d, meta, final_g.reshape(1, d), y, pos)
    return out.reshape(bsz, seq, d)
```

```python
import functools

import jax
import jax.numpy as jnp
from jax import lax
from jax.experimental import pallas as pl
from jax.experimental.pallas import tpu as pltpu

D_MODEL = 1024
SEQ = 2048
DEPTH = 2
POOL_WINDOWS = (2, 4, 8, 16)
POOL_GC = 64
D_POOL = 256
D_SGU = 512
SGU_HEADS = 4
SGU_HD = 128
SGU_BLOCK = 128
CHUNK = 64
D_CONV = 256
CONV_W = 3
D_IN = 2048
D_FF = 3584
N_EXPERTS = 8
EPS = 1e-6

LANES = 128
POOL_HALO = 32
CONV_HALO = 8
TM_MIX = 512
TM_FFN = 1024
TF_FFN = 512
TM_ROUTE = 512
TM_EXP = 1024
TQ_EXP = 256
NQ_EXP = TM_EXP // TQ_EXP
TM_MOVE = 1024
TOP_K = 2
VMEM_LIMIT = 48 * 1024 * 1024
M_I1, M_I2, M_R1, M_R2, M_G1, M_G2 = 0, 1, 2, 3, 4, 5
META_ROWS = 8

C_A, C_U, C_V, C_GB, C_GC, C_XC = 0, 256, 768, 1280, 1536, 1792

F32 = jnp.float32
BF16 = jnp.bfloat16


def _layer_row(ref, layer):
    return ref[layer:layer + 1, :]


def _rms(x, g):
    ms = jnp.mean(x * x, axis=-1, keepdims=True)
    return (x * lax.rsqrt(ms + EPS)) * g


def _mixer_kernel(layer, x_ref, n1g_ref, win_ref, poolw_ref, pscale_ref, lng_ref, lnb_ref, sguw_ref,
                  sgub_ref, convw_ref, gg_ref, wout_ref, o_ref, p_scr, a_scr, s2_scr, s4_scr,
                  s8_scr, z_scr, y_scr):
    tiles_per_seq = SEQ // TM_MIX
    seq_tile = pl.program_id(0) % tiles_per_seq

    @pl.when(seq_tile == 0)
    def _():
        a_scr[0:POOL_HALO, :] = jnp.zeros((POOL_HALO, D_POOL), F32)
        z_scr[0:CONV_HALO, :] = jnp.zeros((CONV_HALO, D_CONV), F32)

    x = x_ref[...]
    hb = _rms(x, _layer_row(n1g_ref, layer)).astype(BF16)

    p_scr[...] = jnp.dot(hb, win_ref[...], preferred_element_type=F32)

    def proj(c0, width):
        return p_scr[:, c0:c0 + width]

    a = proj(C_A, D_POOL)
    rows = POOL_HALO + TM_MIX
    a_scr[POOL_HALO:rows, :] = a
    levels = (a_scr, s2_scr, s4_scr, s8_scr)
    for k in range(1, len(POOL_WINDOWS)):
        w, lo = POOL_WINDOWS[k - 1] // 2, 8 * k
        prev = levels[k - 1]
        levels[k][lo:rows, :] = prev[lo:rows, :] + prev[lo - w:rows - w, :]
    w_last = POOL_WINDOWS[-1] // 2
    s_last = s8_scr[POOL_HALO:rows, :] + s8_scr[POOL_HALO - w_last:rows - w_last, :]
    lane = lax.broadcasted_iota(jnp.int32, (1, D_POOL), 1)
    group = lane // POOL_GC
    wsum = jnp.where(group == 0, s2_scr[POOL_HALO:rows, :],
                     jnp.where(group == 1, s4_scr[POOL_HALO:rows, :],
                               jnp.where(group == 2, s8_scr[POOL_HALO:rows, :], s_last)))
    wlane = jnp.where(group == 0, POOL_WINDOWS[0],
                      jnp.where(group == 1, POOL_WINDOWS[1],
                                jnp.where(group == 2, POOL_WINDOWS[2], POOL_WINDOWS[3])))
    seen = seq_tile * TM_MIX + lax.broadcasted_iota(jnp.int32, (TM_MIX, 1), 0) + 1
    inv_count = jnp.where(seen >= wlane, 1.0 / wlane.astype(F32), 1.0 / seen.astype(F32))
    d = wsum * inv_count - a
    y_a = jnp.dot(d.astype(BF16), poolw_ref[...], preferred_element_type=F32)
    y_a = y_a * _layer_row(pscale_ref, layer)
    a_scr[0:POOL_HALO, :] = a_scr[TM_MIX:rows, :]
    gg = _layer_row(gg_ref, layer)
    y_scr[:, 0:D_POOL] = _rms(y_a, gg[:, 0:D_POOL]).astype(BF16)

    gb = proj(C_GB, D_CONV)
    z = proj(C_GC, D_CONV) * proj(C_XC, D_CONV)
    z_scr[CONV_HALO:CONV_HALO + TM_MIX, :] = z
    cw = convw_ref[...]
    zc = (cw[0:1, :] * z_scr[CONV_HALO - 2:CONV_HALO - 2 + TM_MIX, :]
          + cw[1:2, :] * z_scr[CONV_HALO - 1:CONV_HALO - 1 + TM_MIX, :]
          + cw[2:3, :] * z)
    y_c = gb * zc
    z_scr[0:CONV_HALO, :] = z_scr[TM_MIX:TM_MIX + CONV_HALO, :]
    y_scr[:, D_POOL + D_SGU:] = _rms(y_c, gg[:, D_POOL + D_SGU:]).astype(BF16)

    u = proj(C_U, D_SGU)
    v = proj(C_V, D_SGU)
    mu = jnp.mean(v, axis=-1, keepdims=True)
    vc = v - mu
    var = jnp.mean(vc * vc, axis=-1, keepdims=True)
    vn = ((vc * lax.rsqrt(var + EPS)) * _layer_row(lng_ref, layer)
          + _layer_row(lnb_ref, layer)).astype(BF16)
    ci = lax.broadcasted_iota(jnp.int32, (SGU_BLOCK, SGU_BLOCK), 0) // CHUNK
    cj = lax.broadcasted_iota(jnp.int32, (SGU_BLOCK, SGU_BLOCK), 1) // CHUNK
    mask = (ci >= cj).astype(F32)
    sgub = sgub_ref[...]
    head_cols = []
    for hd in range(SGU_HEADS):
        wm = (sguw_ref[hd] * mask).astype(BF16)
        bias = sgub[:, hd:hd + 1]
        blocks = []
        for blk in range(TM_MIX // SGU_BLOCK):
            vblk = vn[blk * SGU_BLOCK:(blk + 1) * SGU_BLOCK, hd * SGU_HD:(hd + 1) * SGU_HD]
            blocks.append(jnp.dot(wm, vblk, preferred_element_type=F32) + bias)
        head_cols.append(jnp.concatenate(blocks, axis=0))
    mixed = jnp.concatenate(head_cols, axis=1)
    y_b = u * mixed
    y_scr[:, D_POOL:D_POOL + D_SGU] = _rms(y_b, gg[:, D_POOL:D_POOL + D_SGU]).astype(BF16)

    o_ref[...] = x + jnp.dot(y_scr[...], wout_ref[...], preferred_element_type=F32)


def _token_mixer(layer, x2d, n1g, win, poolw_bd, pscale, lng, lnb, sguw, sgub_t, convw, gg, wout):
    n_tok = x2d.shape[0]
    vec = lambda n: pl.BlockSpec((DEPTH, n), lambda i: (0, 0))
    mat = lambda *shape: pl.BlockSpec((None,) + shape, lambda i: (layer,) + (0,) * len(shape))
    return pl.pallas_call(
        functools.partial(_mixer_kernel, layer),
        grid=(n_tok // TM_MIX,),
        in_specs=[
            pl.BlockSpec((TM_MIX, D_MODEL), lambda i: (i, 0)),
            vec(D_MODEL),
            mat(D_MODEL, D_IN),
            mat(D_POOL, D_POOL),
            vec(D_POOL),
            vec(D_SGU),
            vec(D_SGU),
            mat(SGU_HEADS, SGU_BLOCK, SGU_BLOCK),
            mat(SGU_BLOCK, SGU_HEADS),
            mat(CONV_W, D_CONV),
            vec(D_MODEL),
            mat(D_MODEL, D_MODEL),
        ],
        out_specs=pl.BlockSpec((TM_MIX, D_MODEL), lambda i: (i, 0)),
        out_shape=jax.ShapeDtypeStruct((n_tok, D_MODEL), F32),
        scratch_shapes=[
            pltpu.VMEM((TM_MIX, D_IN), F32),
            pltpu.VMEM((POOL_HALO + TM_MIX, D_POOL), F32),
            pltpu.VMEM((POOL_HALO + TM_MIX, D_POOL), F32),
            pltpu.VMEM((POOL_HALO + TM_MIX, D_POOL), F32),
            pltpu.VMEM((POOL_HALO + TM_MIX, D_POOL), F32),
            pltpu.VMEM((CONV_HALO + TM_MIX, D_CONV), F32),
            pltpu.VMEM((TM_MIX, D_MODEL), BF16),
        ],
        compiler_params=pltpu.CompilerParams(
            dimension_semantics=("arbitrary",), vmem_limit_bytes=VMEM_LIMIT),
        name="token_mixer",
    )(x2d, n1g, win, poolw_bd, pscale, lng, lnb, sguw, sgub_t, convw, gg, wout)


def _swiglu_step(hb, wg_ref, wu_ref, wd_ref):
    g = jnp.dot(hb, wg_ref[...].astype(BF16), preferred_element_type=F32)
    u = jnp.dot(hb, wu_ref[...].astype(BF16), preferred_element_type=F32)
    act = (g * jax.nn.sigmoid(g)) * u
    return jnp.dot(act.astype(BF16), wd_ref[...].astype(BF16), preferred_element_type=F32)


def _ffn_kernel(layer, x_ref, n2g_ref, wg_ref, wu_ref, wd_ref, o_ref, hb_scr):
    @pl.when(pl.program_id(1) == 0)
    def _():
        x = x_ref[...]
        hb_scr[...] = _rms(x, _layer_row(n2g_ref, layer)).astype(BF16)
        o_ref[...] = x

    o_ref[...] += _swiglu_step(hb_scr[...], wg_ref, wu_ref, wd_ref)


def _dense_ffn(layer, x2d, n2g, wg, wu, wd):
    n_tok = x2d.shape[0]
    return pl.pallas_call(
        functools.partial(_ffn_kernel, layer),
        grid=(n_tok // TM_FFN, D_FF // TF_FFN),
        in_specs=[
            pl.BlockSpec((TM_FFN, D_MODEL), lambda i, j: (i, 0)),
            pl.BlockSpec((DEPTH, D_MODEL), lambda i, j: (0, 0)),
            pl.BlockSpec((None, D_MODEL, TF_FFN), lambda i, j: (layer // 2, 0, j)),
            pl.BlockSpec((None, D_MODEL, TF_FFN), lambda i, j: (layer // 2, 0, j)),
            pl.BlockSpec((None, TF_FFN, D_MODEL), lambda i, j: (layer // 2, j, 0)),
        ],
        out_specs=pl.BlockSpec((TM_FFN, D_MODEL), lambda i, j: (i, 0)),
        out_shape=jax.ShapeDtypeStruct((n_tok, D_MODEL), F32),
        scratch_shapes=[pltpu.VMEM((TM_FFN, D_MODEL), BF16)],
        compiler_params=pltpu.CompilerParams(
            dimension_semantics=("arbitrary", "arbitrary"), vmem_limit_bytes=VMEM_LIMIT),
        name="dense_ffn",
    )(x2d, n2g, wg, wu, wd)


def _router_kernel(layer, x_ref, n2g_ref, rw_ref, rb_ref, meta_ref, meta_t_ref, cnt_ref):
    @pl.when(pl.program_id(0) == 0)
    def _():
        cnt_ref[...] = jnp.zeros_like(cnt_ref)

    hb = _rms(x_ref[...], _layer_row(n2g_ref, layer)).astype(BF16)
    logits = jnp.dot(hb, rw_ref[...], preferred_element_type=F32) + rb_ref[...]
    lane = lax.broadcasted_iota(jnp.int32, logits.shape, 1).astype(F32)
    m1 = jnp.max(logits, axis=-1, keepdims=True)
    i1 = jnp.min(jnp.where(logits == m1, lane, float(LANES)), axis=-1, keepdims=True)
    rest = jnp.where(lane == i1, -jnp.inf, logits)
    m2 = jnp.max(rest, axis=-1, keepdims=True)
    i2 = jnp.min(jnp.where(rest == m2, lane, float(LANES)), axis=-1, keepdims=True)
    e2 = jnp.exp(m2 - m1)
    denom = 1.0 + e2

    sel = jnp.where((lane == i1) | (lane == i2), 1.0, 0.0)
    r = lax.broadcasted_iota(jnp.int32, (TM_ROUTE, TM_ROUTE), 0)
    c = lax.broadcasted_iota(jnp.int32, (TM_ROUTE, TM_ROUTE), 1)
    earlier = jnp.where(c < r, 1.0, 0.0).astype(BF16)
    before = jnp.dot(earlier, sel.astype(BF16), preferred_element_type=F32) + cnt_ref[...]
    rank1 = jnp.sum(jnp.where(lane == i1, before, 0.0), axis=-1, keepdims=True)
    rank2 = jnp.sum(jnp.where(lane == i2, before, 0.0), axis=-1, keepdims=True)
    cnt_ref[...] += jnp.sum(sel, axis=0, keepdims=True)

    record = jnp.zeros_like(logits)
    for k, val in ((M_I1, i1), (M_I2, i2), (M_R1, rank1), (M_R2, rank2),
                   (M_G1, 1.0 / denom), (M_G2, e2 / denom)):
        record = jnp.where(lane == k, val, record)
    meta_ref[...] = record
    meta_t_ref[...] = record.T[0:META_ROWS, :]


def _router(layer, x2d, n2g, rw_pad, rb_pad):
    n_tok = x2d.shape[0]
    return pl.pallas_call(
        functools.partial(_router_kernel, layer),
        grid=(n_tok // TM_ROUTE,),
        in_specs=[
            pl.BlockSpec((TM_ROUTE, D_MODEL), lambda i: (i, 0)),
            pl.BlockSpec((DEPTH, D_MODEL), lambda i: (0, 0)),
            pl.BlockSpec((D_MODEL, LANES), lambda i: (0, 0)),
            pl.BlockSpec((1, LANES), lambda i: (0, 0)),
        ],
        out_specs=[
            pl.BlockSpec((TM_ROUTE, LANES), lambda i: (i, 0)),
            pl.BlockSpec((META_ROWS, TM_ROUTE), lambda i: (0, i)),
            pl.BlockSpec((1, LANES), lambda i: (0, 0)),
        ],
        out_shape=[
            jax.ShapeDtypeStruct((n_tok, LANES), F32),
            jax.ShapeDtypeStruct((META_ROWS, n_tok), F32),
            jax.ShapeDtypeStruct((1, LANES), F32),
        ],
        compiler_params=pltpu.CompilerParams(
            dimension_semantics=("arbitrary",), vmem_limit_bytes=VMEM_LIMIT),
        name="router",
    )(x2d, n2g, rw_pad, rb_pad)


SUBLANES = 8
TILES_MOVE = TM_MOVE // SUBLANES


def _issue_row_copies(make_copy):
    def issue(tile, carry):
        for sub in range(SUBLANES):
            for k in range(TOP_K):
                make_copy(tile, sub, tile * SUBLANES + sub, k).start(priority=k)
        return carry

    lax.fori_loop(0, TILES_MOVE, issue, 0)


def _dispatch_kernel(zstart_ref, zflag_ref, pos1_ref, pos2_ref, x_ref, xs_ref, x_scr, zero_scr,
                     sem_in, sem_rows, sem_fill):
    pos_refs = (pos1_ref, pos2_ref)
    step = pl.program_id(0)
    cur = step % 2

    def tile_load(tile, b):
        start = tile * TILES_MOVE
        return pltpu.make_async_copy(x_ref.at[pl.ds(start, TILES_MOVE)], x_scr.at[b], sem_in.at[b])

    def wait_rows(b):
        for k in range(TOP_K):
            pltpu.make_async_copy(x_scr.at[b], x_scr.at[b], sem_rows.at[b, k]).wait()

    @pl.when(step == 0)
    def _():
        tile_load(0, 0).start()
        zero_scr[...] = jnp.zeros_like(zero_scr)
        for e in range(2 * N_EXPERTS):
            @pl.when(zflag_ref[e] > 0)
            def _():
                start = pl.multiple_of(zstart_ref[e], TQ_EXP)
                fill = pltpu.make_async_copy(zero_scr, xs_ref.at[pl.ds(start, TQ_EXP)], sem_fill)
                fill.start()
                fill.wait()

    @pl.when(step > 0)
    def _():
        wait_rows(1 - cur)

    @pl.when(step + 1 < pl.num_programs(0))
    def _():
        tile_load(step + 1, 1 - cur).start()

    tile_load(step, cur).wait()

    _issue_row_copies(lambda tile, sub, row, k: pltpu.make_async_copy(
        x_scr.at[cur, tile, pl.ds(sub, 1)], xs_ref.at[pl.ds(pos_refs[k][row], 1)],
        sem_rows.at[cur, k]))

    @pl.when(step == pl.num_programs(0) - 1)
    def _():
        wait_rows(cur)


def _dispatch(x2d, pos, zstart, zflag, n_rows):
    n_tok = x2d.shape[0]
    grid_spec = pltpu.PrefetchScalarGridSpec(
        num_scalar_prefetch=2,
        grid=(n_tok // TM_MOVE,),
        in_specs=[
            pl.BlockSpec((TM_MOVE,), lambda i, zs, zf: (i,), memory_space=pltpu.SMEM),
            pl.BlockSpec((TM_MOVE,), lambda i, zs, zf: (i,), memory_space=pltpu.SMEM),
            pl.BlockSpec(memory_space=pl.ANY),
        ],
        out_specs=pl.BlockSpec(memory_space=pl.ANY),
        scratch_shapes=[pltpu.VMEM((2, TILES_MOVE, SUBLANES, D_MODEL), F32),
                        pltpu.VMEM((TQ_EXP, D_MODEL), F32),
                        pltpu.SemaphoreType.DMA((2,)), pltpu.SemaphoreType.DMA((2, TOP_K)),
                        pltpu.SemaphoreType.DMA(())],
    )
    return pl.pallas_call(
        _dispatch_kernel,
        grid_spec=grid_spec,
        out_shape=jax.ShapeDtypeStruct((n_rows, D_MODEL), F32),
        compiler_params=pltpu.CompilerParams(
            dimension_semantics=("arbitrary",), vmem_limit_bytes=VMEM_LIMIT),
        name="dispatch",
    )(zstart, zflag, pos[0], pos[1], x2d.reshape(n_tok // SUBLANES, SUBLANES, D_MODEL))


def _expert_kernel(layer, se_ref, sq_ref, sn_ref, nv_ref, *refs):
    xq_refs = refs[:NQ_EXP]
    n2g_ref, wg_ref, wu_ref, wd_ref, y_ref, hb_scr, wgb_scr, wub_scr, wdb_scr = refs[NQ_EXP:]
    step = pl.program_id(0)
    n_quarters = sn_ref[step]
    valid = step < nv_ref[0]

    @pl.when(pl.program_id(1) == 0)
    def _():
        for u, xq_ref in enumerate(xq_refs):
            hb_scr[u * TQ_EXP:(u + 1) * TQ_EXP, :] = _rms(
                xq_ref[...], _layer_row(n2g_ref, layer)).astype(BF16)
        y_ref[...] = jnp.zeros_like(y_ref)

    @pl.when(valid & (n_quarters == NQ_EXP))
    def _():
        y = _swiglu_step(hb_scr[...], wg_ref.at[0], wu_ref.at[0], wd_ref.at[0])
        y_ref[...] += y.reshape(NQ_EXP, TQ_EXP, D_MODEL)

    @pl.when(valid & (n_quarters < NQ_EXP))
    def _():
        wgb_scr[...] = wg_ref[0].astype(BF16)
        wub_scr[...] = wu_ref[0].astype(BF16)
        wdb_scr[...] = wd_ref[0].astype(BF16)
        for u in range(NQ_EXP - 1):
            @pl.when(u < n_quarters)
            def _():
                y_ref[u] += _swiglu_step(hb_scr[u * TQ_EXP:(u + 1) * TQ_EXP, :],
                                         wgb_scr, wub_scr, wdb_scr)


def _expert_ffn(layer, xs, n2g, wg, wu, wd, step_expert, step_quarter, step_count, n_valid):
    n_steps = step_expert.shape[0]
    n_j = D_FF // TF_FFN
    moe = layer // 2

    def ff_block(i, j, nv):
        return jnp.where(i < nv[0], j, n_j - 1)

    def quarter_spec(u):
        return pl.BlockSpec(
            (TQ_EXP, D_MODEL),
            lambda i, j, se, sq, sn, nv: (sq[i] + jnp.minimum(u, sn[i] - 1), 0))

    grid_spec = pltpu.PrefetchScalarGridSpec(
        num_scalar_prefetch=4,
        grid=(n_steps, n_j),
        in_specs=[quarter_spec(u) for u in range(NQ_EXP)] + [
            pl.BlockSpec((DEPTH, D_MODEL), lambda i, j, se, sq, sn, nv: (0, 0)),
            pl.BlockSpec((None, 1, D_MODEL, TF_FFN),
                         lambda i, j, se, sq, sn, nv: (moe, se[i], 0, ff_block(i, j, nv))),
            pl.BlockSpec((None, 1, D_MODEL, TF_FFN),
                         lambda i, j, se, sq, sn, nv: (moe, se[i], 0, ff_block(i, j, nv))),
            pl.BlockSpec((None, 1, TF_FFN, D_MODEL),
                         lambda i, j, se, sq, sn, nv: (moe, se[i], ff_block(i, j, nv), 0)),
        ],
        out_specs=pl.BlockSpec((NQ_EXP, TQ_EXP, D_MODEL), lambda i, j, se, sq, sn, nv: (0, i, 0)),
        scratch_shapes=[pltpu.VMEM((TM_EXP, D_MODEL), BF16),
                        pltpu.VMEM((D_MODEL, TF_FFN), BF16), pltpu.VMEM((D_MODEL, TF_FFN), BF16),
                        pltpu.VMEM((TF_FFN, D_MODEL), BF16)],
    )
    return pl.pallas_call(
        functools.partial(_expert_kernel, layer),
        grid_spec=grid_spec,
        out_shape=jax.ShapeDtypeStruct((NQ_EXP, n_steps * TQ_EXP, D_MODEL), F32),
        compiler_params=pltpu.CompilerParams(
            dimension_semantics=("arbitrary", "arbitrary"), vmem_limit_bytes=VMEM_LIMIT),
        name="expert_ffn",
    )(step_expert, step_quarter, step_count, n_valid, *([xs] * NQ_EXP), n2g, wg, wu, wd)


def _combine_kernel(pos1_ref, pos2_ref, pos1_next_ref, pos2_next_ref, x_ref, meta_ref, fg_ref,
                    y_ref, o_ref, y_scr, sem):
    step = pl.program_id(0)
    cur = step % 2

    def gather(pos_refs, b):
        _issue_row_copies(lambda tile, sub, row, k: pltpu.make_async_copy(
            y_ref.at[pl.ds(pos_refs[k][row], 1)], y_scr.at[b, k, tile, pl.ds(sub, 1)],
            sem.at[b, k]))

    @pl.when(step == 0)
    def _():
        gather((pos1_ref, pos2_ref), 0)

    @pl.when(step + 1 < pl.num_programs(0))
    def _():
        gather((pos1_next_ref, pos2_next_ref), 1 - cur)

    for k in range(TOP_K):
        pltpu.make_async_copy(y_scr.at[cur, k], y_scr.at[cur, k], sem.at[cur, k]).wait()

    meta = meta_ref[...]
    rows = lambda k: y_scr[cur, k].reshape(TM_MOVE, D_MODEL)
    moe = meta[:, M_G1:M_G1 + 1] * rows(0) + meta[:, M_G2:M_G2 + 1] * rows(1)
    o_ref[...] = _rms(x_ref[...] + moe, fg_ref[...])


def _combine(x2d, meta, fg, y, pos):
    n_tok = x2d.shape[0]
    n_steps = n_tok // TM_MOVE
    next_block = lambda i: (jnp.minimum(i + 1, n_steps - 1),)
    return pl.pallas_call(
        _combine_kernel,
        grid=(n_steps,),
        in_specs=[
            pl.BlockSpec((TM_MOVE,), lambda i: (i,), memory_space=pltpu.SMEM),
            pl.BlockSpec((TM_MOVE,), lambda i: (i,), memory_space=pltpu.SMEM),
            pl.BlockSpec((TM_MOVE,), next_block, memory_space=pltpu.SMEM),
            pl.BlockSpec((TM_MOVE,), next_block, memory_space=pltpu.SMEM),
            pl.BlockSpec((TM_MOVE, D_MODEL), lambda i: (i, 0)),
            pl.BlockSpec((TM_MOVE, LANES), lambda i: (i, 0)),
            pl.BlockSpec((1, D_MODEL), lambda i: (0, 0)),
            pl.BlockSpec(memory_space=pl.ANY),
        ],
        out_specs=pl.BlockSpec((TM_MOVE, D_MODEL), lambda i: (i, 0)),
        out_shape=jax.ShapeDtypeStruct((n_tok, D_MODEL), F32),
        scratch_shapes=[pltpu.VMEM((2, TOP_K, TILES_MOVE, SUBLANES, D_MODEL), F32),
                        pltpu.SemaphoreType.DMA((2, TOP_K))],
        compiler_params=pltpu.CompilerParams(
            dimension_semantics=("arbitrary",), vmem_limit_bytes=VMEM_LIMIT),
        name="combine",
    )(pos[0], pos[1], pos[0], pos[1], x2d, meta, fg, y)


def _routing_tables(meta_t, counts_f, n_steps, n_quarters):
    i32 = jnp.int32
    experts = jnp.arange(N_EXPERTS, dtype=i32)
    counts = counts_f[0, :N_EXPERTS].astype(i32)
    quarters = (counts + TQ_EXP - 1) // TQ_EXP
    q_end = jnp.cumsum(quarters)
    q_start = q_end - quarters
    steps = (quarters + NQ_EXP - 1) // NQ_EXP
    s_end = jnp.cumsum(steps)
    s_start = s_end - steps
    n_valid = s_end[-1]

    idx = meta_t[M_I1:M_I2 + 1].astype(i32)
    rank = meta_t[M_R1:M_R2 + 1].astype(i32)
    slot_base = jnp.zeros_like(rank)
    step_base = jnp.zeros_like(rank)
    for e in range(N_EXPERTS):
        slot_base = slot_base + jnp.where(idx == e, q_start[e] * TQ_EXP, 0)
        step_base = step_base + jnp.where(idx == e, s_start[e], 0)
    slot = slot_base + rank
    local_q = rank // TQ_EXP
    y_row = ((local_q % NQ_EXP) * (n_steps * TQ_EXP) + (step_base + local_q // NQ_EXP) * TQ_EXP
             + rank % TQ_EXP)

    step = jnp.minimum(jnp.arange(n_steps, dtype=i32), n_valid - 1)
    step_expert = jnp.minimum(jnp.sum(s_end[None, :] <= step[:, None], axis=-1), N_EXPERTS - 1)
    pick = lambda table: jnp.sum(jnp.where(step_expert[:, None] == experts, table, 0), axis=-1)
    local_step = step - pick(s_start)
    step_quarter = pick(q_start) + NQ_EXP * local_step
    step_count = jnp.minimum(NQ_EXP, pick(quarters) - NQ_EXP * local_step)

    tail = jnp.arange(n_quarters - N_EXPERTS, n_quarters, dtype=i32)
    zstart = jnp.concatenate([(q_end - 1) * TQ_EXP, tail * TQ_EXP])
    zflag = jnp.concatenate([quarters > 0, tail >= q_end[-1]])
    return (slot.astype(i32), y_row.astype(i32), step_expert.astype(i32), step_quarter.astype(i32),
            step_count.astype(i32), n_valid.reshape(1).astype(i32), zstart.astype(i32),
            zflag.astype(i32))


def _block_diag(blocks):
    layers, n, r, c = blocks.shape
    eye = jnp.eye(n, dtype=blocks.dtype)
    return (eye[None, :, None, :, None] * blocks[:, :, :, None, :]).reshape(layers, n * r, n * c)


def kernel(x, norm1_g, w_in, pool_w, pool_scale, sgu_ln_g, sgu_ln_b, sgu_w, sgu_b, conv_w,
           group_g, w_out, norm2_g, ffn_w_gate, ffn_w_up, ffn_w_down, router_w, router_b,
           moe_w_gate, moe_w_up, moe_w_down, final_g):
    bsz, seq, d = x.shape
    assert (seq, d) == (SEQ, D_MODEL) and DEPTH == 2
    x2d = x.reshape(bsz * seq, d)
    mixer_params = (norm1_g, w_in.astype(BF16), _block_diag(pool_w).astype(BF16), pool_scale,
                    sgu_ln_g, sgu_ln_b, sgu_w, sgu_b.transpose(0, 2, 1), conv_w, group_g,
                    w_out.astype(BF16))

    x2d = _token_mixer(0, x2d, *mixer_params)
    x2d = _dense_ffn(0, x2d, norm2_g, ffn_w_gate, ffn_w_up, ffn_w_down)
    x2d = _token_mixer(1, x2d, *mixer_params)
    rw_pad = jnp.zeros((d, LANES), BF16).at[:, :N_EXPERTS].set(router_w[0].astype(BF16))
    rb_pad = jnp.full((1, LANES), -1e30, F32).at[0, :N_EXPERTS].set(router_b[0])
    meta, meta_t, counts = _router(1, x2d, norm2_g, rw_pad, rb_pad)
    n_steps = (TOP_K * bsz * seq) // TM_EXP + N_EXPERTS
    n_quarters = (TOP_K * bsz * seq) // TQ_EXP + N_EXPERTS
    slot, y_row, step_expert, step_quarter, step_count, n_valid, zstart, zflag = _routing_tables(
        meta_t, counts, n_steps, n_quarters)
    xs = _dispatch(x2d, slot, zstart, zflag, n_quarters * TQ_EXP)
    y = _expert_ffn(1, xs, norm2_g, moe_w_gate, moe_w_up, moe_w_down, step_expert, step_quarter,
                    step_count, n_valid)
    out = _combine(x2d, meta, final_g.reshape(1, d), y.reshape(-1, d), y_row)
    return out.reshape(bsz, seq, d)
```

```python
import functools

import jax
import jax.numpy as jnp
from jax import lax
from jax.experimental import pallas as pl
from jax.experimental.pallas import tpu as pltpu

D_MODEL = 1024
SEQ = 2048
DEPTH = 2
POOL_WINDOWS = (2, 4, 8, 16)
POOL_GC = 64
D_POOL = 256
D_SGU = 512
SGU_HEADS = 4
SGU_HD = 128
SGU_BLOCK = 128
CHUNK = 64
D_CONV = 256
CONV_W = 3
D_IN = 2048
D_FF = 3584
N_EXPERTS = 8
EPS = 1e-6

LANES = 128
POOL_HALO = 32
CONV_HALO = 8
TM_MIX = 512
TM_FFN = 1024
TF_FFN = 512
TM_ROUTE = 512
TM_EXP = 1024
TQ_EXP = 256
NQ_EXP = TM_EXP // TQ_EXP
TM_MOVE = 1024
TOP_K = 2
VMEM_LIMIT = 48 * 1024 * 1024
M_I1, M_I2, M_R1, M_R2, M_G1, M_G2 = 0, 1, 2, 3, 4, 5
META_ROWS = 8

C_A, C_U, C_V, C_GB, C_GC, C_XC = 0, 256, 768, 1280, 1536, 1792

F32 = jnp.float32
BF16 = jnp.bfloat16


def _layer_row(ref, layer):
    return ref[layer:layer + 1, :]


def _rms(x, g):
    ms = jnp.mean(x * x, axis=-1, keepdims=True)
    return (x * lax.rsqrt(ms + EPS)) * g


def _mixer_kernel(layer, x_ref, n1g_ref, win_ref, poolw_ref, pscale_ref, lng_ref, lnb_ref, sguw_ref,
                  sgub_ref, convw_ref, gg_ref, wout_ref, o_ref, p_scr, a_scr, s2_scr, s4_scr,
                  s8_scr, z_scr, y_scr):
    tiles_per_seq = SEQ // TM_MIX
    seq_tile = pl.program_id(0) % tiles_per_seq

    @pl.when(seq_tile == 0)
    def _():
        a_scr[0:POOL_HALO, :] = jnp.zeros((POOL_HALO, D_POOL), F32)
        z_scr[0:CONV_HALO, :] = jnp.zeros((CONV_HALO, D_CONV), F32)

    x = x_ref[...]
    hb = _rms(x, _layer_row(n1g_ref, layer)).astype(BF16)

    p_scr[...] = jnp.dot(hb, win_ref[...], preferred_element_type=F32)

    def proj(c0, width):
        return p_scr[:, c0:c0 + width]

    a = proj(C_A, D_POOL)
    rows = POOL_HALO + TM_MIX
    a_scr[POOL_HALO:rows, :] = a
    levels = (a_scr, s2_scr, s4_scr, s8_scr)
    for k in range(1, len(POOL_WINDOWS)):
        w, lo = POOL_WINDOWS[k - 1] // 2, 8 * k
        prev = levels[k - 1]
        levels[k][lo:rows, :] = prev[lo:rows, :] + prev[lo - w:rows - w, :]
    w_last = POOL_WINDOWS[-1] // 2
    s_last = s8_scr[POOL_HALO:rows, :] + s8_scr[POOL_HALO - w_last:rows - w_last, :]
    lane = lax.broadcasted_iota(jnp.int32, (1, D_POOL), 1)
    group = lane // POOL_GC
    wsum = jnp.where(group == 0, s2_scr[POOL_HALO:rows, :],
                     jnp.where(group == 1, s4_scr[POOL_HALO:rows, :],
                               jnp.where(group == 2, s8_scr[POOL_HALO:rows, :], s_last)))
    wlane = jnp.where(group == 0, POOL_WINDOWS[0],
                      jnp.where(group == 1, POOL_WINDOWS[1],
                                jnp.where(group == 2, POOL_WINDOWS[2], POOL_WINDOWS[3])))
    seen = seq_tile * TM_MIX + lax.broadcasted_iota(jnp.int32, (TM_MIX, 1), 0) + 1
    inv_count = jnp.where(seen >= wlane, 1.0 / wlane.astype(F32), 1.0 / seen.astype(F32))
    d = wsum * inv_count - a
    y_a = jnp.dot(d.astype(BF16), poolw_ref[...], preferred_element_type=F32)
    y_a = y_a * _layer_row(pscale_ref, layer)
    a_scr[0:POOL_HALO, :] = a_scr[TM_MIX:rows, :]
    gg = _layer_row(gg_ref, layer)
    y_scr[:, 0:D_POOL] = _rms(y_a, gg[:, 0:D_POOL]).astype(BF16)

    gb = proj(C_GB, D_CONV)
    z = proj(C_GC, D_CONV) * proj(C_XC, D_CONV)
    z_scr[CONV_HALO:CONV_HALO + TM_MIX, :] = z
    cw = convw_ref[...]
    zc = (cw[0:1, :] * z_scr[CONV_HALO - 2:CONV_HALO - 2 + TM_MIX, :]
          + cw[1:2, :] * z_scr[CONV_HALO - 1:CONV_HALO - 1 + TM_MIX, :]
          + cw[2:3, :] * z)
    y_c = gb * zc
    z_scr[0:CONV_HALO, :] = z_scr[TM_MIX:TM_MIX + CONV_HALO, :]
    y_scr[:, D_POOL + D_SGU:] = _rms(y_c, gg[:, D_POOL + D_SGU:]).astype(BF16)

    u = proj(C_U, D_SGU)
    v = proj(C_V, D_SGU)
    mu = jnp.mean(v, axis=-1, keepdims=True)
    vc = v - mu
    var = jnp.mean(vc * vc, axis=-1, keepdims=True)
    vn = ((vc * lax.rsqrt(var + EPS)) * _layer_row(lng_ref, layer)
          + _layer_row(lnb_ref, layer)).astype(BF16)
    ci = lax.broadcasted_iota(jnp.int32, (SGU_BLOCK, SGU_BLOCK), 0) // CHUNK
    cj = lax.broadcasted_iota(jnp.int32, (SGU_BLOCK, SGU_BLOCK), 1) // CHUNK
    mask = (ci >= cj).astype(F32)
    sgub = sgub_ref[...]
    head_cols = []
    for hd in range(SGU_HEADS):
        wm = (sguw_ref[hd] * mask).astype(BF16)
        bias = sgub[:, hd:hd + 1]
        blocks = []
        for blk in range(TM_MIX // SGU_BLOCK):
            vblk = vn[blk * SGU_BLOCK:(blk + 1) * SGU_BLOCK, hd * SGU_HD:(hd + 1) * SGU_HD]
            blocks.append(jnp.dot(wm, vblk, preferred_element_type=F32) + bias)
        head_cols.append(jnp.concatenate(blocks, axis=0))
    mixed = jnp.concatenate(head_cols, axis=1)
    y_b = u * mixed
    y_scr[:, D_POOL:D_POOL + D_SGU] = _rms(y_b, gg[:, D_POOL:D_POOL + D_SGU]).astype(BF16)

    o_ref[...] = x + jnp.dot(y_scr[...], wout_ref[...], preferred_element_type=F32)


def _token_mixer(layer, x2d, n1g, win, poolw_bd, pscale, lng, lnb, sguw, sgub_t, convw, gg, wout):
    n_tok = x2d.shape[0]
    vec = lambda n: pl.BlockSpec((DEPTH, n), lambda i: (0, 0))
    mat = lambda *shape: pl.BlockSpec((None,) + shape, lambda i: (layer,) + (0,) * len(shape))
    return pl.pallas_call(
        functools.partial(_mixer_kernel, layer),
        grid=(n_tok // TM_MIX,),
        in_specs=[
            pl.BlockSpec((TM_MIX, D_MODEL), lambda i: (i, 0)),
            vec(D_MODEL),
            mat(D_MODEL, D_IN),
            mat(D_POOL, D_POOL),
            vec(D_POOL),
            vec(D_SGU),
            vec(D_SGU),
            mat(SGU_HEADS, SGU_BLOCK, SGU_BLOCK),
            mat(SGU_BLOCK, SGU_HEADS),
            mat(CONV_W, D_CONV),
            vec(D_MODEL),
            mat(D_MODEL, D_MODEL),
        ],
        out_specs=pl.BlockSpec((TM_MIX, D_MODEL), lambda i: (i, 0)),
        out_shape=jax.ShapeDtypeStruct((n_tok, D_MODEL), F32),
        scratch_shapes=[
            pltpu.VMEM((TM_MIX, D_IN), F32),
            pltpu.VMEM((POOL_HALO + TM_MIX, D_POOL), F32),
            pltpu.VMEM((POOL_HALO + TM_MIX, D_POOL), F32),
            pltpu.VMEM((POOL_HALO + TM_MIX, D_POOL), F32),
            pltpu.VMEM((POOL_HALO + TM_MIX, D_POOL), F32),
            pltpu.VMEM((CONV_HALO + TM_MIX, D_CONV), F32),
            pltpu.VMEM((TM_MIX, D_MODEL), BF16),
        ],
        compiler_params=pltpu.CompilerParams(
            dimension_semantics=("arbitrary",), vmem_limit_bytes=VMEM_LIMIT),
        name="token_mixer",
    )(x2d, n1g, win, poolw_bd, pscale, lng, lnb, sguw, sgub_t, convw, gg, wout)


def _swiglu_step(hb, wg_ref, wu_ref, wd_ref):
    g = jnp.dot(hb, wg_ref[...].astype(BF16), preferred_element_type=F32)
    u = jnp.dot(hb, wu_ref[...].astype(BF16), preferred_element_type=F32)
    act = (g * jax.nn.sigmoid(g)) * u
    return jnp.dot(act.astype(BF16), wd_ref[...].astype(BF16), preferred_element_type=F32)


def _ffn_kernel(layer, x_ref, n2g_ref, wg_ref, wu_ref, wd_ref, o_ref, hb_scr):
    @pl.when(pl.program_id(1) == 0)
    def _():
        x = x_ref[...]
        hb_scr[...] = _rms(x, _layer_row(n2g_ref, layer)).astype(BF16)
        o_ref[...] = x

    o_ref[...] += _swiglu_step(hb_scr[...], wg_ref, wu_ref, wd_ref)


def _dense_ffn(layer, x2d, n2g, wg, wu, wd):
    n_tok = x2d.shape[0]
    return pl.pallas_call(
        functools.partial(_ffn_kernel, layer),
        grid=(n_tok // TM_FFN, D_FF // TF_FFN),
        in_specs=[
            pl.BlockSpec((TM_FFN, D_MODEL), lambda i, j: (i, 0)),
            pl.BlockSpec((DEPTH, D_MODEL), lambda i, j: (0, 0)),
            pl.BlockSpec((None, D_MODEL, TF_FFN), lambda i, j: (layer // 2, 0, j)),
            pl.BlockSpec((None, D_MODEL, TF_FFN), lambda i, j: (layer // 2, 0, j)),
            pl.BlockSpec((None, TF_FFN, D_MODEL), lambda i, j: (layer // 2, j, 0)),
        ],
        out_specs=pl.BlockSpec((TM_FFN, D_MODEL), lambda i, j: (i, 0)),
        out_shape=jax.ShapeDtypeStruct((n_tok, D_MODEL), F32),
        scratch_shapes=[pltpu.VMEM((TM_FFN, D_MODEL), BF16)],
        compiler_params=pltpu.CompilerParams(
            dimension_semantics=("arbitrary", "arbitrary"), vmem_limit_bytes=VMEM_LIMIT),
        name="dense_ffn",
    )(x2d, n2g, wg, wu, wd)


def _router_kernel(layer, x_ref, n2g_ref, rw_ref, rb_ref, meta_ref, meta_t_ref, cnt_ref):
    @pl.when(pl.program_id(0) == 0)
    def _():
        cnt_ref[...] = jnp.zeros_like(cnt_ref)

    hb = _rms(x_ref[...], _layer_row(n2g_ref, layer)).astype(BF16)
    logits = jnp.dot(hb, rw_ref[...], preferred_element_type=F32) + rb_ref[...]
    lane = lax.broadcasted_iota(jnp.int32, logits.shape, 1).astype(F32)
    m1 = jnp.max(logits, axis=-1, keepdims=True)
    i1 = jnp.min(jnp.where(logits == m1, lane, float(LANES)), axis=-1, keepdims=True)
    rest = jnp.where(lane == i1, -jnp.inf, logits)
    m2 = jnp.max(rest, axis=-1, keepdims=True)
    i2 = jnp.min(jnp.where(rest == m2, lane, float(LANES)), axis=-1, keepdims=True)
    e2 = jnp.exp(m2 - m1)
    denom = 1.0 + e2

    sel = jnp.where((lane == i1) | (lane == i2), 1.0, 0.0)
    r = lax.broadcasted_iota(jnp.int32, (TM_ROUTE, TM_ROUTE), 0)
    c = lax.broadcasted_iota(jnp.int32, (TM_ROUTE, TM_ROUTE), 1)
    earlier = jnp.where(c < r, 1.0, 0.0).astype(BF16)
    before = jnp.dot(earlier, sel.astype(BF16), preferred_element_type=F32) + cnt_ref[...]
    rank1 = jnp.sum(jnp.where(lane == i1, before, 0.0), axis=-1, keepdims=True)
    rank2 = jnp.sum(jnp.where(lane == i2, before, 0.0), axis=-1, keepdims=True)
    cnt_ref[...] += jnp.sum(sel, axis=0, keepdims=True)

    record = jnp.zeros_like(logits)
    for k, val in ((M_I1, i1), (M_I2, i2), (M_R1, rank1), (M_R2, rank2),
                   (M_G1, 1.0 / denom), (M_G2, e2 / denom)):
        record = jnp.where(lane == k, val, record)
    meta_ref[...] = record
    meta_t_ref[...] = record.T[0:META_ROWS, :]


def _router(layer, x2d, n2g, rw_pad, rb_pad):
    n_tok = x2d.shape[0]
    return pl.pallas_call(
        functools.partial(_router_kernel, layer),
        grid=(n_tok // TM_ROUTE,),
        in_specs=[
            pl.BlockSpec((TM_ROUTE, D_MODEL), lambda i: (i, 0)),
            pl.BlockSpec((DEPTH, D_MODEL), lambda i: (0, 0)),
            pl.BlockSpec((D_MODEL, LANES), lambda i: (0, 0)),
            pl.BlockSpec((1, LANES), lambda i: (0, 0)),
        ],
        out_specs=[
            pl.BlockSpec((TM_ROUTE, LANES), lambda i: (i, 0)),
            pl.BlockSpec((META_ROWS, TM_ROUTE), lambda i: (0, i)),
            pl.BlockSpec((1, LANES), lambda i: (0, 0)),
        ],
        out_shape=[
            jax.ShapeDtypeStruct((n_tok, LANES), F32),
            jax.ShapeDtypeStruct((META_ROWS, n_tok), F32),
            jax.ShapeDtypeStruct((1, LANES), F32),
        ],
        compiler_params=pltpu.CompilerParams(
            dimension_semantics=("arbitrary",), vmem_limit_bytes=VMEM_LIMIT),
        name="router",
    )(x2d, n2g, rw_pad, rb_pad)


SUBLANES = 8
TILES_MOVE = TM_MOVE // SUBLANES


def _issue_row_copies(make_copy):
    def issue(tile, carry):
        for sub in range(SUBLANES):
            for k in range(TOP_K):
                make_copy(tile, sub, tile * SUBLANES + sub, k).start(priority=k)
        return carry

    lax.fori_loop(0, TILES_MOVE, issue, 0)


def _dispatch_kernel(zstart_ref, zflag_ref, pos1_ref, pos2_ref, x_ref, xs_ref, x_scr, zero_scr,
                     sem_in, sem_rows, sem_fill):
    pos_refs = (pos1_ref, pos2_ref)
    step = pl.program_id(0)
    cur = step % 2

    def tile_load(tile, b):
        start = tile * TILES_MOVE
        return pltpu.make_async_copy(x_ref.at[pl.ds(start, TILES_MOVE)], x_scr.at[b], sem_in.at[b])

    def wait_rows(b):
        for k in range(TOP_K):
            pltpu.make_async_copy(x_scr.at[b], x_scr.at[b], sem_rows.at[b, k]).wait()

    @pl.when(step == 0)
    def _():
        tile_load(0, 0).start()
        zero_scr[...] = jnp.zeros_like(zero_scr)
        for e in range(2 * N_EXPERTS):
            @pl.when(zflag_ref[e] > 0)
            def _():
                start = pl.multiple_of(zstart_ref[e], TQ_EXP)
                fill = pltpu.make_async_copy(zero_scr, xs_ref.at[pl.ds(start, TQ_EXP)], sem_fill)
                fill.start()
                fill.wait()

    @pl.when(step > 0)
    def _():
        wait_rows(1 - cur)

    @pl.when(step + 1 < pl.num_programs(0))
    def _():
        tile_load(step + 1, 1 - cur).start()

    tile_load(step, cur).wait()

    _issue_row_copies(lambda tile, sub, row, k: pltpu.make_async_copy(
        x_scr.at[cur, tile, pl.ds(sub, 1)], xs_ref.at[pl.ds(pos_refs[k][row], 1)],
        sem_rows.at[cur, k]))

    @pl.when(step == pl.num_programs(0) - 1)
    def _():
        wait_rows(cur)


def _dispatch(x2d, pos, zstart, zflag, n_rows):
    n_tok = x2d.shape[0]
    grid_spec = pltpu.PrefetchScalarGridSpec(
        num_scalar_prefetch=2,
        grid=(n_tok // TM_MOVE,),
        in_specs=[
            pl.BlockSpec((TM_MOVE,), lambda i, zs, zf: (i,), memory_space=pltpu.SMEM),
            pl.BlockSpec((TM_MOVE,), lambda i, zs, zf: (i,), memory_space=pltpu.SMEM),
            pl.BlockSpec(memory_space=pl.ANY),
        ],
        out_specs=pl.BlockSpec(memory_space=pl.ANY),
        scratch_shapes=[pltpu.VMEM((2, TILES_MOVE, SUBLANES, D_MODEL), F32),
                        pltpu.VMEM((TQ_EXP, D_MODEL), F32),
                        pltpu.SemaphoreType.DMA((2,)), pltpu.SemaphoreType.DMA((2, TOP_K)),
                        pltpu.SemaphoreType.DMA(())],
    )
    return pl.pallas_call(
        _dispatch_kernel,
        grid_spec=grid_spec,
        out_shape=jax.ShapeDtypeStruct((n_rows, D_MODEL), F32),
        compiler_params=pltpu.CompilerParams(
            dimension_semantics=("arbitrary",), vmem_limit_bytes=VMEM_LIMIT),
        name="dispatch",
    )(zstart, zflag, pos[0], pos[1], x2d.reshape(n_tok // SUBLANES, SUBLANES, D_MODEL))


def _expert_kernel(layer, se_ref, sq_ref, sn_ref, nv_ref, *refs):
    xq_refs = refs[:NQ_EXP]
    n2g_ref, wg_ref, wu_ref, wd_ref, y_ref, hb_scr, wgb_scr, wub_scr, wdb_scr = refs[NQ_EXP:]
    step = pl.program_id(0)
    n_quarters = sn_ref[step]
    valid = step < nv_ref[0]

    @pl.when(pl.program_id(1) == 0)
    def _():
        for u, xq_ref in enumerate(xq_refs):
            hb_scr[u * TQ_EXP:(u + 1) * TQ_EXP, :] = _rms(
                xq_ref[...], _layer_row(n2g_ref, layer)).astype(BF16)
        y_ref[...] = jnp.zeros_like(y_ref)

    @pl.when(valid & (n_quarters == NQ_EXP))
    def _():
        y_ref[...] += _swiglu_step(hb_scr[...], wg_ref.at[0], wu_ref.at[0], wd_ref.at[0])

    @pl.when(valid & (n_quarters < NQ_EXP))
    def _():
        wgb_scr[...] = wg_ref[0].astype(BF16)
        wub_scr[...] = wu_ref[0].astype(BF16)
        wdb_scr[...] = wd_ref[0].astype(BF16)
        for u in range(NQ_EXP - 1):
            @pl.when(u < n_quarters)
            def _():
                quarter = slice(u * TQ_EXP, (u + 1) * TQ_EXP)
                y_ref[quarter, :] += _swiglu_step(hb_scr[quarter, :], wgb_scr, wub_scr, wdb_scr)


def _expert_ffn(layer, xs, n2g, wg, wu, wd, step_expert, step_quarter, step_count, n_valid):
    n_steps = step_expert.shape[0]
    n_j = D_FF // TF_FFN
    moe = layer // 2

    def ff_block(i, j, nv):
        return jnp.where(i < nv[0], j, n_j - 1)

    def quarter_spec(u):
        return pl.BlockSpec(
            (TQ_EXP, D_MODEL),
            lambda i, j, se, sq, sn, nv: (sq[i] + jnp.minimum(u, sn[i] - 1), 0))

    grid_spec = pltpu.PrefetchScalarGridSpec(
        num_scalar_prefetch=4,
        grid=(n_steps, n_j),
        in_specs=[quarter_spec(u) for u in range(NQ_EXP)] + [
            pl.BlockSpec((DEPTH, D_MODEL), lambda i, j, se, sq, sn, nv: (0, 0)),
            pl.BlockSpec((None, 1, D_MODEL, TF_FFN),
                         lambda i, j, se, sq, sn, nv: (moe, se[i], 0, ff_block(i, j, nv))),
            pl.BlockSpec((None, 1, D_MODEL, TF_FFN),
                         lambda i, j, se, sq, sn, nv: (moe, se[i], 0, ff_block(i, j, nv))),
            pl.BlockSpec((None, 1, TF_FFN, D_MODEL),
                         lambda i, j, se, sq, sn, nv: (moe, se[i], ff_block(i, j, nv), 0)),
        ],
        out_specs=pl.BlockSpec((TM_EXP, D_MODEL), lambda i, j, se, sq, sn, nv: (i, 0)),
        scratch_shapes=[pltpu.VMEM((TM_EXP, D_MODEL), BF16),
                        pltpu.VMEM((D_MODEL, TF_FFN), BF16), pltpu.VMEM((D_MODEL, TF_FFN), BF16),
                        pltpu.VMEM((TF_FFN, D_MODEL), BF16)],
    )
    return pl.pallas_call(
        functools.partial(_expert_kernel, layer),
        grid_spec=grid_spec,
        out_shape=jax.ShapeDtypeStruct((n_steps * TM_EXP, D_MODEL), F32),
        compiler_params=pltpu.CompilerParams(
            dimension_semantics=("arbitrary", "arbitrary"), vmem_limit_bytes=VMEM_LIMIT),
        name="expert_ffn",
    )(step_expert, step_quarter, step_count, n_valid, *([xs] * NQ_EXP), n2g, wg, wu, wd)


def _combine_kernel(pos1_ref, pos2_ref, pos1_next_ref, pos2_next_ref, x_ref, meta_ref, fg_ref,
                    y_ref, o_ref, y_scr, sem):
    step = pl.program_id(0)
    cur = step % 2

    def gather(pos_refs, b):
        _issue_row_copies(lambda tile, sub, row, k: pltpu.make_async_copy(
            y_ref.at[pl.ds(pos_refs[k][row], 1)], y_scr.at[b, k, tile, pl.ds(sub, 1)],
            sem.at[b, k]))

    @pl.when(step == 0)
    def _():
        gather((pos1_ref, pos2_ref), 0)

    @pl.when(step + 1 < pl.num_programs(0))
    def _():
        gather((pos1_next_ref, pos2_next_ref), 1 - cur)

    for k in range(TOP_K):
        pltpu.make_async_copy(y_scr.at[cur, k], y_scr.at[cur, k], sem.at[cur, k]).wait()

    meta = meta_ref[...]
    rows = lambda k: y_scr[cur, k].reshape(TM_MOVE, D_MODEL)
    moe = meta[:, M_G1:M_G1 + 1] * rows(0) + meta[:, M_G2:M_G2 + 1] * rows(1)
    o_ref[...] = _rms(x_ref[...] + moe, fg_ref[...])


def _combine(x2d, meta, fg, y, pos):
    n_tok = x2d.shape[0]
    n_steps = n_tok // TM_MOVE
    next_block = lambda i: (jnp.minimum(i + 1, n_steps - 1),)
    return pl.pallas_call(
        _combine_kernel,
        grid=(n_steps,),
        in_specs=[
            pl.BlockSpec((TM_MOVE,), lambda i: (i,), memory_space=pltpu.SMEM),
            pl.BlockSpec((TM_MOVE,), lambda i: (i,), memory_space=pltpu.SMEM),
            pl.BlockSpec((TM_MOVE,), next_block, memory_space=pltpu.SMEM),
            pl.BlockSpec((TM_MOVE,), next_block, memory_space=pltpu.SMEM),
            pl.BlockSpec((TM_MOVE, D_MODEL), lambda i: (i, 0)),
            pl.BlockSpec((TM_MOVE, LANES), lambda i: (i, 0)),
            pl.BlockSpec((1, D_MODEL), lambda i: (0, 0)),
            pl.BlockSpec(memory_space=pl.ANY),
        ],
        out_specs=pl.BlockSpec((TM_MOVE, D_MODEL), lambda i: (i, 0)),
        out_shape=jax.ShapeDtypeStruct((n_tok, D_MODEL), F32),
        scratch_shapes=[pltpu.VMEM((2, TOP_K, TILES_MOVE, SUBLANES, D_MODEL), F32),
                        pltpu.SemaphoreType.DMA((2, TOP_K))],
        compiler_params=pltpu.CompilerParams(
            dimension_semantics=("arbitrary",), vmem_limit_bytes=VMEM_LIMIT),
        name="combine",
    )(pos[0], pos[1], pos[0], pos[1], x2d, meta, fg, y)


def _routing_tables(meta_t, counts_f, n_steps, n_quarters):
    i32 = jnp.int32
    experts = jnp.arange(N_EXPERTS, dtype=i32)
    counts = counts_f[0, :N_EXPERTS].astype(i32)
    quarters = (counts + TQ_EXP - 1) // TQ_EXP
    q_end = jnp.cumsum(quarters)
    q_start = q_end - quarters
    steps = (quarters + NQ_EXP - 1) // NQ_EXP
    s_end = jnp.cumsum(steps)
    s_start = s_end - steps
    n_valid = s_end[-1]

    idx = meta_t[M_I1:M_I2 + 1].astype(i32)
    rank = meta_t[M_R1:M_R2 + 1].astype(i32)
    slot_base = jnp.zeros_like(rank)
    step_base = jnp.zeros_like(rank)
    for e in range(N_EXPERTS):
        slot_base = slot_base + jnp.where(idx == e, q_start[e] * TQ_EXP, 0)
        step_base = step_base + jnp.where(idx == e, s_start[e], 0)
    slot = slot_base + rank
    local_q = rank // TQ_EXP
    y_row = (step_base + local_q // NQ_EXP) * TM_EXP + rank % TM_EXP

    step = jnp.minimum(jnp.arange(n_steps, dtype=i32), n_valid - 1)
    step_expert = jnp.minimum(jnp.sum(s_end[None, :] <= step[:, None], axis=-1), N_EXPERTS - 1)
    pick = lambda table: jnp.sum(jnp.where(step_expert[:, None] == experts, table, 0), axis=-1)
    local_step = step - pick(s_start)
    step_quarter = pick(q_start) + NQ_EXP * local_step
    step_count = jnp.minimum(NQ_EXP, pick(quarters) - NQ_EXP * local_step)

    tail = jnp.arange(n_quarters - N_EXPERTS, n_quarters, dtype=i32)
    zstart = jnp.concatenate([(q_end - 1) * TQ_EXP, tail * TQ_EXP])
    zflag = jnp.concatenate([quarters > 0, tail >= q_end[-1]])
    return (slot.astype(i32), y_row.astype(i32), step_expert.astype(i32), step_quarter.astype(i32),
            step_count.astype(i32), n_valid.reshape(1).astype(i32), zstart.astype(i32),
            zflag.astype(i32))


def _block_diag(blocks):
    layers, n, r, c = blocks.shape
    eye = jnp.eye(n, dtype=blocks.dtype)
    return (eye[None, :, None, :, None] * blocks[:, :, :, None, :]).reshape(layers, n * r, n * c)


def kernel(x, norm1_g, w_in, pool_w, pool_scale, sgu_ln_g, sgu_ln_b, sgu_w, sgu_b, conv_w,
           group_g, w_out, norm2_g, ffn_w_gate, ffn_w_up, ffn_w_down, router_w, router_b,
           moe_w_gate, moe_w_up, moe_w_down, final_g):
    bsz, seq, d = x.shape
    assert (seq, d) == (SEQ, D_MODEL) and DEPTH == 2
    x2d = x.reshape(bsz * seq, d)
    mixer_params = (norm1_g, w_in.astype(BF16), _block_diag(pool_w).astype(BF16), pool_scale,
                    sgu_ln_g, sgu_ln_b, sgu_w, sgu_b.transpose(0, 2, 1), conv_w, group_g,
                    w_out.astype(BF16))

    x2d = _token_mixer(0, x2d, *mixer_params)
    x2d = _dense_ffn(0, x2d, norm2_g, ffn_w_gate, ffn_w_up, ffn_w_down)
    x2d = _token_mixer(1, x2d, *mixer_params)
    rw_pad = jnp.zeros((d, LANES), BF16).at[:, :N_EXPERTS].set(router_w[0].astype(BF16))
    rb_pad = jnp.full((1, LANES), -1e30, F32).at[0, :N_EXPERTS].set(router_b[0])
    meta, meta_t, counts = _router(1, x2d, norm2_g, rw_pad, rb_pad)
    n_steps = (TOP_K * bsz * seq) // TM_EXP + N_EXPERTS
    n_quarters = (TOP_K * bsz * seq) // TQ_EXP + N_EXPERTS
    slot, y_row, step_expert, step_quarter, step_count, n_valid, zstart, zflag = _routing_tables(
        meta_t, counts, n_steps, n_quarters)
    xs = _dispatch(x2d, slot, zstart, zflag, n_quarters * TQ_EXP)
    y = _expert_ffn(1, xs, norm2_g, moe_w_gate, moe_w_up, moe_w_down, step_expert, step_quarter,
                    step_count, n_valid)
    out = _combine(x2d, meta, final_g.reshape(1, d), y, y_row)
    return out.reshape(bsz, seq, d)
```

```python
import functools

import jax
import jax.numpy as jnp
from jax import lax
from jax.experimental import pallas as pl
from jax.experimental.pallas import tpu as pltpu

D_MODEL = 1024
SEQ = 2048
DEPTH = 2
POOL_WINDOWS = (2, 4, 8, 16)
POOL_GC = 64
D_POOL = 256
D_SGU = 512
SGU_HEADS = 4
SGU_HD = 128
SGU_BLOCK = 128
CHUNK = 64
D_CONV = 256
CONV_W = 3
D_IN = 2048
D_FF = 3584
N_EXPERTS = 8
EPS = 1e-6

LANES = 128
POOL_HALO = 32
CONV_HALO = 8
TM_MIX = 512
TM_FFN = 1024
TF_FFN = 512
TM_ROUTE = 512
TM_EXP = 1024
TQ_EXP = 256
NQ_EXP = TM_EXP // TQ_EXP
TM_MOVE = 1024
TOP_K = 2
VMEM_LIMIT = 48 * 1024 * 1024
M_I1, M_I2, M_R1, M_R2, M_G1, M_G2 = 0, 1, 2, 3, 4, 5
META_ROWS = 8

C_A, C_U, C_V, C_GB, C_GC, C_XC = 0, 256, 768, 1280, 1536, 1792

F32 = jnp.float32
BF16 = jnp.bfloat16


def _layer_row(ref, layer):
    return ref[layer:layer + 1, :]


def _rms(x, g):
    ms = jnp.mean(x * x, axis=-1, keepdims=True)
    return (x * lax.rsqrt(ms + EPS)) * g


def _mixer_kernel(layer, x_ref, n1g_ref, win_ref, poolw_ref, pscale_ref, lng_ref, lnb_ref, sguw_ref,
                  sgub_ref, convw_ref, gg_ref, wout_ref, o_ref, p_scr, a_scr, s2_scr, s4_scr,
                  s8_scr, z_scr, y_scr):
    tiles_per_seq = SEQ // TM_MIX
    seq_tile = pl.program_id(0) % tiles_per_seq

    @pl.when(seq_tile == 0)
    def _():
        a_scr[0:POOL_HALO, :] = jnp.zeros((POOL_HALO, D_POOL), F32)
        z_scr[0:CONV_HALO, :] = jnp.zeros((CONV_HALO, D_CONV), F32)

    x = x_ref[...]
    hb = _rms(x, _layer_row(n1g_ref, layer)).astype(BF16)

    p_scr[...] = jnp.dot(hb, win_ref[...], preferred_element_type=F32)

    def proj(c0, width):
        return p_scr[:, c0:c0 + width]

    a = proj(C_A, D_POOL)
    rows = POOL_HALO + TM_MIX
    a_scr[POOL_HALO:rows, :] = a
    levels = (a_scr, s2_scr, s4_scr, s8_scr)
    for k in range(1, len(POOL_WINDOWS)):
        w, lo = POOL_WINDOWS[k - 1] // 2, 8 * k
        prev = levels[k - 1]
        levels[k][lo:rows, :] = prev[lo:rows, :] + prev[lo - w:rows - w, :]
    w_last = POOL_WINDOWS[-1] // 2
    s_last = s8_scr[POOL_HALO:rows, :] + s8_scr[POOL_HALO - w_last:rows - w_last, :]
    lane = lax.broadcasted_iota(jnp.int32, (1, D_POOL), 1)
    group = lane // POOL_GC
    wsum = jnp.where(group == 0, s2_scr[POOL_HALO:rows, :],
                     jnp.where(group == 1, s4_scr[POOL_HALO:rows, :],
                               jnp.where(group == 2, s8_scr[POOL_HALO:rows, :], s_last)))
    wlane = jnp.where(group == 0, POOL_WINDOWS[0],
                      jnp.where(group == 1, POOL_WINDOWS[1],
                                jnp.where(group == 2, POOL_WINDOWS[2], POOL_WINDOWS[3])))
    seen = seq_tile * TM_MIX + lax.broadcasted_iota(jnp.int32, (TM_MIX, 1), 0) + 1
    inv_count = jnp.where(seen >= wlane, 1.0 / wlane.astype(F32), 1.0 / seen.astype(F32))
    d = wsum * inv_count - a
    y_a = jnp.dot(d.astype(BF16), poolw_ref[...], preferred_element_type=F32)
    y_a = y_a * _layer_row(pscale_ref, layer)
    a_scr[0:POOL_HALO, :] = a_scr[TM_MIX:rows, :]
    gg = _layer_row(gg_ref, layer)
    y_scr[:, 0:D_POOL] = _rms(y_a, gg[:, 0:D_POOL]).astype(BF16)

    gb = proj(C_GB, D_CONV)
    z = proj(C_GC, D_CONV) * proj(C_XC, D_CONV)
    z_scr[CONV_HALO:CONV_HALO + TM_MIX, :] = z
    cw = convw_ref[...]
    zc = (cw[0:1, :] * z_scr[CONV_HALO - 2:CONV_HALO - 2 + TM_MIX, :]
          + cw[1:2, :] * z_scr[CONV_HALO - 1:CONV_HALO - 1 + TM_MIX, :]
          + cw[2:3, :] * z)
    y_c = gb * zc
    z_scr[0:CONV_HALO, :] = z_scr[TM_MIX:TM_MIX + CONV_HALO, :]
    y_scr[:, D_POOL + D_SGU:] = _rms(y_c, gg[:, D_POOL + D_SGU:]).astype(BF16)

    u = proj(C_U, D_SGU)
    v = proj(C_V, D_SGU)
    mu = jnp.mean(v, axis=-1, keepdims=True)
    vc = v - mu
    var = jnp.mean(vc * vc, axis=-1, keepdims=True)
    vn = ((vc * lax.rsqrt(var + EPS)) * _layer_row(lng_ref, layer)
          + _layer_row(lnb_ref, layer)).astype(BF16)
    ci = lax.broadcasted_iota(jnp.int32, (SGU_BLOCK, SGU_BLOCK), 0) // CHUNK
    cj = lax.broadcasted_iota(jnp.int32, (SGU_BLOCK, SGU_BLOCK), 1) // CHUNK
    mask = (ci >= cj).astype(F32)
    sgub = sgub_ref[...]
    head_cols = []
    for hd in range(SGU_HEADS):
        wm = (sguw_ref[hd] * mask).astype(BF16)
        bias = sgub[:, hd:hd + 1]
        blocks = []
        for blk in range(TM_MIX // SGU_BLOCK):
            vblk = vn[blk * SGU_BLOCK:(blk + 1) * SGU_BLOCK, hd * SGU_HD:(hd + 1) * SGU_HD]
            blocks.append(jnp.dot(wm, vblk, preferred_element_type=F32) + bias)
        head_cols.append(jnp.concatenate(blocks, axis=0))
    mixed = jnp.concatenate(head_cols, axis=1)
    y_b = u * mixed
    y_scr[:, D_POOL:D_POOL + D_SGU] = _rms(y_b, gg[:, D_POOL:D_POOL + D_SGU]).astype(BF16)

    o_ref[...] = x + jnp.dot(y_scr[...], wout_ref[...], preferred_element_type=F32)


def _token_mixer(layer, x2d, n1g, win, poolw_bd, pscale, lng, lnb, sguw, sgub_t, convw, gg, wout):
    n_tok = x2d.shape[0]
    vec = lambda n: pl.BlockSpec((DEPTH, n), lambda i: (0, 0))
    mat = lambda *shape: pl.BlockSpec((None,) + shape, lambda i: (layer,) + (0,) * len(shape))
    return pl.pallas_call(
        functools.partial(_mixer_kernel, layer),
        grid=(n_tok // TM_MIX,),
        in_specs=[
            pl.BlockSpec((TM_MIX, D_MODEL), lambda i: (i, 0)),
            vec(D_MODEL),
            mat(D_MODEL, D_IN),
            mat(D_POOL, D_POOL),
            vec(D_POOL),
            vec(D_SGU),
            vec(D_SGU),
            mat(SGU_HEADS, SGU_BLOCK, SGU_BLOCK),
            mat(SGU_BLOCK, SGU_HEADS),
            mat(CONV_W, D_CONV),
            vec(D_MODEL),
            mat(D_MODEL, D_MODEL),
        ],
        out_specs=pl.BlockSpec((TM_MIX, D_MODEL), lambda i: (i, 0)),
        out_shape=jax.ShapeDtypeStruct((n_tok, D_MODEL), F32),
        scratch_shapes=[
            pltpu.VMEM((TM_MIX, D_IN), F32),
            pltpu.VMEM((POOL_HALO + TM_MIX, D_POOL), F32),
            pltpu.VMEM((POOL_HALO + TM_MIX, D_POOL), F32),
            pltpu.VMEM((POOL_HALO + TM_MIX, D_POOL), F32),
            pltpu.VMEM((POOL_HALO + TM_MIX, D_POOL), F32),
            pltpu.VMEM((CONV_HALO + TM_MIX, D_CONV), F32),
            pltpu.VMEM((TM_MIX, D_MODEL), BF16),
        ],
        compiler_params=pltpu.CompilerParams(
            dimension_semantics=("arbitrary",), vmem_limit_bytes=VMEM_LIMIT),
        name="token_mixer",
    )(x2d, n1g, win, poolw_bd, pscale, lng, lnb, sguw, sgub_t, convw, gg, wout)


def _swiglu_step(hb, wg_ref, wu_ref, wd_ref):
    g = jnp.dot(hb, wg_ref[...].astype(BF16), preferred_element_type=F32)
    u = jnp.dot(hb, wu_ref[...].astype(BF16), preferred_element_type=F32)
    act = (g * jax.nn.sigmoid(g)) * u
    return jnp.dot(act.astype(BF16), wd_ref[...].astype(BF16), preferred_element_type=F32)


def _ffn_kernel(layer, x_ref, n2g_ref, wg_ref, wu_ref, wd_ref, o_ref, hb_scr):
    @pl.when(pl.program_id(1) == 0)
    def _():
        x = x_ref[...]
        hb_scr[...] = _rms(x, _layer_row(n2g_ref, layer)).astype(BF16)
        o_ref[...] = x

    o_ref[...] += _swiglu_step(hb_scr[...], wg_ref, wu_ref, wd_ref)


def _dense_ffn(layer, x2d, n2g, wg, wu, wd):
    n_tok = x2d.shape[0]
    return pl.pallas_call(
        functools.partial(_ffn_kernel, layer),
        grid=(n_tok // TM_FFN, D_FF // TF_FFN),
        in_specs=[
            pl.BlockSpec((TM_FFN, D_MODEL), lambda i, j: (i, 0)),
            pl.BlockSpec((DEPTH, D_MODEL), lambda i, j: (0, 0)),
            pl.BlockSpec((None, D_MODEL, TF_FFN), lambda i, j: (layer // 2, 0, j)),
            pl.BlockSpec((None, D_MODEL, TF_FFN), lambda i, j: (layer // 2, 0, j)),
            pl.BlockSpec((None, TF_FFN, D_MODEL), lambda i, j: (layer // 2, j, 0)),
        ],
        out_specs=pl.BlockSpec((TM_FFN, D_MODEL), lambda i, j: (i, 0)),
        out_shape=jax.ShapeDtypeStruct((n_tok, D_MODEL), F32),
        scratch_shapes=[pltpu.VMEM((TM_FFN, D_MODEL), BF16)],
        compiler_params=pltpu.CompilerParams(
            dimension_semantics=("arbitrary", "arbitrary"), vmem_limit_bytes=VMEM_LIMIT),
        name="dense_ffn",
    )(x2d, n2g, wg, wu, wd)


def _router_kernel(layer, x_ref, n2g_ref, rw_ref, rb_ref, h_ref, meta_ref, meta_t_ref, cnt_ref):
    @pl.when(pl.program_id(0) == 0)
    def _():
        cnt_ref[...] = jnp.zeros_like(cnt_ref)

    h = _rms(x_ref[...], _layer_row(n2g_ref, layer))
    h_ref[...] = h
    hb = h.astype(BF16)
    logits = jnp.dot(hb, rw_ref[...], preferred_element_type=F32) + rb_ref[...]
    lane = lax.broadcasted_iota(jnp.int32, logits.shape, 1).astype(F32)
    m1 = jnp.max(logits, axis=-1, keepdims=True)
    i1 = jnp.min(jnp.where(logits == m1, lane, float(LANES)), axis=-1, keepdims=True)
    rest = jnp.where(lane == i1, -jnp.inf, logits)
    m2 = jnp.max(rest, axis=-1, keepdims=True)
    i2 = jnp.min(jnp.where(rest == m2, lane, float(LANES)), axis=-1, keepdims=True)
    e2 = jnp.exp(m2 - m1)
    denom = 1.0 + e2

    sel = jnp.where((lane == i1) | (lane == i2), 1.0, 0.0)
    r = lax.broadcasted_iota(jnp.int32, (TM_ROUTE, TM_ROUTE), 0)
    c = lax.broadcasted_iota(jnp.int32, (TM_ROUTE, TM_ROUTE), 1)
    earlier = jnp.where(c < r, 1.0, 0.0).astype(BF16)
    before = jnp.dot(earlier, sel.astype(BF16), preferred_element_type=F32) + cnt_ref[...]
    rank1 = jnp.sum(jnp.where(lane == i1, before, 0.0), axis=-1, keepdims=True)
    rank2 = jnp.sum(jnp.where(lane == i2, before, 0.0), axis=-1, keepdims=True)
    cnt_ref[...] += jnp.sum(sel, axis=0, keepdims=True)

    record = jnp.zeros_like(logits)
    for k, val in ((M_I1, i1), (M_I2, i2), (M_R1, rank1), (M_R2, rank2),
                   (M_G1, 1.0 / denom), (M_G2, e2 / denom)):
        record = jnp.where(lane == k, val, record)
    meta_ref[...] = record
    meta_t_ref[...] = record.T[0:META_ROWS, :]


def _router(layer, x2d, n2g, rw_pad, rb_pad):
    n_tok = x2d.shape[0]
    return pl.pallas_call(
        functools.partial(_router_kernel, layer),
        grid=(n_tok // TM_ROUTE,),
        in_specs=[
            pl.BlockSpec((TM_ROUTE, D_MODEL), lambda i: (i, 0)),
            pl.BlockSpec((DEPTH, D_MODEL), lambda i: (0, 0)),
            pl.BlockSpec((D_MODEL, LANES), lambda i: (0, 0)),
            pl.BlockSpec((1, LANES), lambda i: (0, 0)),
        ],
        out_specs=[
            pl.BlockSpec((TM_ROUTE, D_MODEL), lambda i: (i, 0)),
            pl.BlockSpec((TM_ROUTE, LANES), lambda i: (i, 0)),
            pl.BlockSpec((META_ROWS, TM_ROUTE), lambda i: (0, i)),
            pl.BlockSpec((1, LANES), lambda i: (0, 0)),
        ],
        out_shape=[
            jax.ShapeDtypeStruct((n_tok, D_MODEL), F32),
            jax.ShapeDtypeStruct((n_tok, LANES), F32),
            jax.ShapeDtypeStruct((META_ROWS, n_tok), F32),
            jax.ShapeDtypeStruct((1, LANES), F32),
        ],
        compiler_params=pltpu.CompilerParams(
            dimension_semantics=("arbitrary",), vmem_limit_bytes=VMEM_LIMIT),
        name="router",
    )(x2d, n2g, rw_pad, rb_pad)


SUBLANES = 8
TILES_MOVE = TM_MOVE // SUBLANES


def _issue_row_copies(make_copy):
    def issue(tile, carry):
        for sub in range(SUBLANES):
            for k in range(TOP_K):
                make_copy(tile, sub, tile * SUBLANES + sub, k).start(priority=k)
        return carry

    lax.fori_loop(0, TILES_MOVE, issue, 0)


def _dispatch_kernel(zstart_ref, zflag_ref, pos1_ref, pos2_ref, x_ref, xs_ref, x_scr, zero_scr,
                     sem_in, sem_rows, sem_fill):
    pos_refs = (pos1_ref, pos2_ref)
    step = pl.program_id(0)
    cur = step % 2

    def tile_load(tile, b):
        start = tile * TILES_MOVE
        return pltpu.make_async_copy(x_ref.at[pl.ds(start, TILES_MOVE)], x_scr.at[b], sem_in.at[b])

    def wait_rows(b):
        for k in range(TOP_K):
            pltpu.make_async_copy(x_scr.at[b], x_scr.at[b], sem_rows.at[b, k]).wait()

    @pl.when(step == 0)
    def _():
        tile_load(0, 0).start()
        zero_scr[...] = jnp.zeros_like(zero_scr)
        for e in range(2 * N_EXPERTS):
            @pl.when(zflag_ref[e] > 0)
            def _():
                start = pl.multiple_of(zstart_ref[e], TQ_EXP)
                fill = pltpu.make_async_copy(zero_scr, xs_ref.at[pl.ds(start, TQ_EXP)], sem_fill)
                fill.start()
                fill.wait()

    @pl.when(step > 0)
    def _():
        wait_rows(1 - cur)

    @pl.when(step + 1 < pl.num_programs(0))
    def _():
        tile_load(step + 1, 1 - cur).start()

    tile_load(step, cur).wait()

    _issue_row_copies(lambda tile, sub, row, k: pltpu.make_async_copy(
        x_scr.at[cur, tile, pl.ds(sub, 1)], xs_ref.at[pl.ds(pos_refs[k][row], 1)],
        sem_rows.at[cur, k]))

    @pl.when(step == pl.num_programs(0) - 1)
    def _():
        wait_rows(cur)


def _dispatch(x2d, pos, zstart, zflag, n_rows):
    n_tok = x2d.shape[0]
    grid_spec = pltpu.PrefetchScalarGridSpec(
        num_scalar_prefetch=2,
        grid=(n_tok // TM_MOVE,),
        in_specs=[
            pl.BlockSpec((TM_MOVE,), lambda i, zs, zf: (i,), memory_space=pltpu.SMEM),
            pl.BlockSpec((TM_MOVE,), lambda i, zs, zf: (i,), memory_space=pltpu.SMEM),
            pl.BlockSpec(memory_space=pl.ANY),
        ],
        out_specs=pl.BlockSpec(memory_space=pl.ANY),
        scratch_shapes=[pltpu.VMEM((2, TILES_MOVE, SUBLANES, D_MODEL), F32),
                        pltpu.VMEM((TQ_EXP, D_MODEL), F32),
                        pltpu.SemaphoreType.DMA((2,)), pltpu.SemaphoreType.DMA((2, TOP_K)),
                        pltpu.SemaphoreType.DMA(())],
    )
    return pl.pallas_call(
        _dispatch_kernel,
        grid_spec=grid_spec,
        out_shape=jax.ShapeDtypeStruct((n_rows, D_MODEL), F32),
        compiler_params=pltpu.CompilerParams(
            dimension_semantics=("arbitrary",), vmem_limit_bytes=VMEM_LIMIT),
        name="dispatch",
    )(zstart, zflag, pos[0], pos[1], x2d.reshape(n_tok // SUBLANES, SUBLANES, D_MODEL))


def _expert_kernel(se_ref, sq_ref, sn_ref, nv_ref, *refs):
    xq_refs = refs[:NQ_EXP]
    wg_ref, wu_ref, wd_ref, y_ref, hb_scr, wgb_scr, wub_scr, wdb_scr = refs[NQ_EXP:]
    step = pl.program_id(0)
    n_quarters = sn_ref[step]
    valid = step < nv_ref[0]

    @pl.when(pl.program_id(1) == 0)
    def _():
        for u, xq_ref in enumerate(xq_refs):
            hb_scr[u * TQ_EXP:(u + 1) * TQ_EXP, :] = xq_ref[...].astype(BF16)
        y_ref[...] = jnp.zeros_like(y_ref)

    @pl.when(valid & (n_quarters == NQ_EXP))
    def _():
        y_ref[...] += _swiglu_step(hb_scr[...], wg_ref.at[0], wu_ref.at[0], wd_ref.at[0])

    @pl.when(valid & (n_quarters < NQ_EXP))
    def _():
        wgb_scr[...] = wg_ref[0].astype(BF16)
        wub_scr[...] = wu_ref[0].astype(BF16)
        wdb_scr[...] = wd_ref[0].astype(BF16)
        for u in range(NQ_EXP - 1):
            @pl.when(u < n_quarters)
            def _():
                quarter = slice(u * TQ_EXP, (u + 1) * TQ_EXP)
                y_ref[quarter, :] += _swiglu_step(hb_scr[quarter, :], wgb_scr, wub_scr, wdb_scr)


def _expert_ffn(layer, xs, wg, wu, wd, step_expert, step_quarter, step_count, n_valid):
    n_steps = step_expert.shape[0]
    n_j = D_FF // TF_FFN
    moe = layer // 2

    def ff_block(i, j, nv):
        return jnp.where(i < nv[0], j, n_j - 1)

    def quarter_spec(u):
        return pl.BlockSpec(
            (TQ_EXP, D_MODEL),
            lambda i, j, se, sq, sn, nv: (sq[i] + jnp.minimum(u, sn[i] - 1), 0))

    grid_spec = pltpu.PrefetchScalarGridSpec(
        num_scalar_prefetch=4,
        grid=(n_steps, n_j),
        in_specs=[quarter_spec(u) for u in range(NQ_EXP)] + [
            pl.BlockSpec((None, 1, D_MODEL, TF_FFN),
                         lambda i, j, se, sq, sn, nv: (moe, se[i], 0, ff_block(i, j, nv))),
            pl.BlockSpec((None, 1, D_MODEL, TF_FFN),
                         lambda i, j, se, sq, sn, nv: (moe, se[i], 0, ff_block(i, j, nv))),
            pl.BlockSpec((None, 1, TF_FFN, D_MODEL),
                         lambda i, j, se, sq, sn, nv: (moe, se[i], ff_block(i, j, nv), 0)),
        ],
        out_specs=pl.BlockSpec((TM_EXP, D_MODEL), lambda i, j, se, sq, sn, nv: (i, 0)),
        scratch_shapes=[pltpu.VMEM((TM_EXP, D_MODEL), BF16),
                        pltpu.VMEM((D_MODEL, TF_FFN), BF16), pltpu.VMEM((D_MODEL, TF_FFN), BF16),
                        pltpu.VMEM((TF_FFN, D_MODEL), BF16)],
    )
    return pl.pallas_call(
        _expert_kernel,
        grid_spec=grid_spec,
        out_shape=jax.ShapeDtypeStruct((n_steps * TM_EXP, D_MODEL), F32),
        compiler_params=pltpu.CompilerParams(
            dimension_semantics=("arbitrary", "arbitrary"), vmem_limit_bytes=VMEM_LIMIT),
        name="expert_ffn",
    )(step_expert, step_quarter, step_count, n_valid, *([xs] * NQ_EXP), wg, wu, wd)


def _combine_kernel(pos1_ref, pos2_ref, pos1_next_ref, pos2_next_ref, x_ref, meta_ref, fg_ref,
                    y_ref, o_ref, y_scr, sem):
    step = pl.program_id(0)
    cur = step % 2

    def gather(pos_refs, b):
        _issue_row_copies(lambda tile, sub, row, k: pltpu.make_async_copy(
            y_ref.at[pl.ds(pos_refs[k][row], 1)], y_scr.at[b, k, tile, pl.ds(sub, 1)],
            sem.at[b, k]))

    @pl.when(step == 0)
    def _():
        gather((pos1_ref, pos2_ref), 0)

    @pl.when(step + 1 < pl.num_programs(0))
    def _():
        gather((pos1_next_ref, pos2_next_ref), 1 - cur)

    for k in range(TOP_K):
        pltpu.make_async_copy(y_scr.at[cur, k], y_scr.at[cur, k], sem.at[cur, k]).wait()

    meta = meta_ref[...]
    rows = lambda k: y_scr[cur, k].reshape(TM_MOVE, D_MODEL)
    moe = meta[:, M_G1:M_G1 + 1] * rows(0) + meta[:, M_G2:M_G2 + 1] * rows(1)
    o_ref[...] = _rms(x_ref[...] + moe, fg_ref[...])


def _combine(x2d, meta, fg, y, pos):
    n_tok = x2d.shape[0]
    n_steps = n_tok // TM_MOVE
    next_block = lambda i: (jnp.minimum(i + 1, n_steps - 1),)
    return pl.pallas_call(
        _combine_kernel,
        grid=(n_steps,),
        in_specs=[
            pl.BlockSpec((TM_MOVE,), lambda i: (i,), memory_space=pltpu.SMEM),
            pl.BlockSpec((TM_MOVE,), lambda i: (i,), memory_space=pltpu.SMEM),
            pl.BlockSpec((TM_MOVE,), next_block, memory_space=pltpu.SMEM),
            pl.BlockSpec((TM_MOVE,), next_block, memory_space=pltpu.SMEM),
            pl.BlockSpec((TM_MOVE, D_MODEL), lambda i: (i, 0)),
            pl.BlockSpec((TM_MOVE, LANES), lambda i: (i, 0)),
            pl.BlockSpec((1, D_MODEL), lambda i: (0, 0)),
            pl.BlockSpec(memory_space=pl.ANY),
        ],
        out_specs=pl.BlockSpec((TM_MOVE, D_MODEL), lambda i: (i, 0)),
        out_shape=jax.ShapeDtypeStruct((n_tok, D_MODEL), F32),
        scratch_shapes=[pltpu.VMEM((2, TOP_K, TILES_MOVE, SUBLANES, D_MODEL), F32),
                        pltpu.SemaphoreType.DMA((2, TOP_K))],
        compiler_params=pltpu.CompilerParams(
            dimension_semantics=("arbitrary",), vmem_limit_bytes=VMEM_LIMIT),
        name="combine",
    )(pos[0], pos[1], pos[0], pos[1], x2d, meta, fg, y)


def _routing_tables(meta_t, counts_f, n_steps, n_quarters):
    i32 = jnp.int32
    experts = jnp.arange(N_EXPERTS, dtype=i32)
    counts = counts_f[0, :N_EXPERTS].astype(i32)
    quarters = (counts + TQ_EXP - 1) // TQ_EXP
    q_end = jnp.cumsum(quarters)
    q_start = q_end - quarters
    steps = (quarters + NQ_EXP - 1) // NQ_EXP
    s_end = jnp.cumsum(steps)
    s_start = s_end - steps
    n_valid = s_end[-1]

    idx = meta_t[M_I1:M_I2 + 1].astype(i32)
    rank = meta_t[M_R1:M_R2 + 1].astype(i32)
    slot_base = jnp.zeros_like(rank)
    step_base = jnp.zeros_like(rank)
    for e in range(N_EXPERTS):
        slot_base = slot_base + jnp.where(idx == e, q_start[e] * TQ_EXP, 0)
        step_base = step_base + jnp.where(idx == e, s_start[e], 0)
    slot = slot_base + rank
    local_q = rank // TQ_EXP
    y_row = (step_base + local_q // NQ_EXP) * TM_EXP + rank % TM_EXP

    step = jnp.minimum(jnp.arange(n_steps, dtype=i32), n_valid - 1)
    step_expert = jnp.minimum(jnp.sum(s_end[None, :] <= step[:, None], axis=-1), N_EXPERTS - 1)
    pick = lambda table: jnp.sum(jnp.where(step_expert[:, None] == experts, table, 0), axis=-1)
    local_step = step - pick(s_start)
    step_quarter = pick(q_start) + NQ_EXP * local_step
    step_count = jnp.minimum(NQ_EXP, pick(quarters) - NQ_EXP * local_step)

    tail = jnp.arange(n_quarters - N_EXPERTS, n_quarters, dtype=i32)
    zstart = jnp.concatenate([(q_end - 1) * TQ_EXP, tail * TQ_EXP])
    zflag = jnp.concatenate([quarters > 0, tail >= q_end[-1]])
    return (slot.astype(i32), y_row.astype(i32), step_expert.astype(i32), step_quarter.astype(i32),
            step_count.astype(i32), n_valid.reshape(1).astype(i32), zstart.astype(i32),
            zflag.astype(i32))


def _block_diag(blocks):
    layers, n, r, c = blocks.shape
    eye = jnp.eye(n, dtype=blocks.dtype)
    return (eye[None, :, None, :, None] * blocks[:, :, :, None, :]).reshape(layers, n * r, n * c)


def kernel(x, norm1_g, w_in, pool_w, pool_scale, sgu_ln_g, sgu_ln_b, sgu_w, sgu_b, conv_w,
           group_g, w_out, norm2_g, ffn_w_gate, ffn_w_up, ffn_w_down, router_w, router_b,
           moe_w_gate, moe_w_up, moe_w_down, final_g):
    bsz, seq, d = x.shape
    assert (seq, d) == (SEQ, D_MODEL) and DEPTH == 2
    x2d = x.reshape(bsz * seq, d)
    mixer_params = (norm1_g, w_in.astype(BF16), _block_diag(pool_w).astype(BF16), pool_scale,
                    sgu_ln_g, sgu_ln_b, sgu_w, sgu_b.transpose(0, 2, 1), conv_w, group_g,
                    w_out.astype(BF16))

    x2d = _token_mixer(0, x2d, *mixer_params)
    x2d = _dense_ffn(0, x2d, norm2_g, ffn_w_gate, ffn_w_up, ffn_w_down)
    x2d = _token_mixer(1, x2d, *mixer_params)
    rw_pad = jnp.zeros((d, LANES), BF16).at[:, :N_EXPERTS].set(router_w[0].astype(BF16))
    rb_pad = jnp.full((1, LANES), -1e30, F32).at[0, :N_EXPERTS].set(router_b[0])
    h, meta, meta_t, counts = _router(1, x2d, norm2_g, rw_pad, rb_pad)
    n_steps = (TOP_K * bsz * seq) // TM_EXP + N_EXPERTS
    n_quarters = (TOP_K * bsz * seq) // TQ_EXP + N_EXPERTS
    slot, y_row, step_expert, step_quarter, step_count, n_valid, zstart, zflag = _routing_tables(
        meta_t, counts, n_steps, n_quarters)
    xs = _dispatch(h, slot, zstart, zflag, n_quarters * TQ_EXP)
    y = _expert_ffn(1, xs, moe_w_gate, moe_w_up, moe_w_down, step_expert, step_quarter,
                    step_count, n_valid)
    out = _combine(x2d, meta, final_g.reshape(1, d), y, y_row)
    return out.reshape(bsz, seq, d)
```

```python
import functools

import jax
import jax.numpy as jnp
from jax import lax
from jax.experimental import pallas as pl
from jax.experimental.pallas import tpu as pltpu

D_MODEL = 1024
SEQ = 2048
DEPTH = 2
POOL_WINDOWS = (2, 4, 8, 16)
POOL_GC = 64
D_POOL = 256
D_SGU = 512
SGU_HEADS = 4
SGU_HD = 128
SGU_BLOCK = 128
CHUNK = 64
D_CONV = 256
CONV_W = 3
D_IN = 2048
D_FF = 3584
N_EXPERTS = 8
EPS = 1e-6

LANES = 128
POOL_HALO = 32
CONV_HALO = 8
TM_MIX = 512
TM_FFN = 1024
TF_FFN = 512
TM_ROUTE = 512
TM_EXP = 1024
TQ_EXP = 256
NQ_EXP = TM_EXP // TQ_EXP
TM_MOVE = 1024
TOP_K = 2
VMEM_LIMIT = 48 * 1024 * 1024
M_I1, M_I2, M_R1, M_R2, M_G1, M_G2 = 0, 1, 2, 3, 4, 5
META_ROWS = 8

C_A, C_U, C_V, C_GB, C_GC, C_XC = 0, 256, 768, 1280, 1536, 1792

F32 = jnp.float32
BF16 = jnp.bfloat16


def _layer_row(ref, layer):
    return ref[layer:layer + 1, :]


def _rms(x, g):
    ms = jnp.mean(x * x, axis=-1, keepdims=True)
    return (x * lax.rsqrt(ms + EPS)) * g


def _mixer_kernel(layer, x_ref, n1g_ref, win_ref, poolw_ref, pscale_ref, lng_ref, lnb_ref, sguw_ref,
                  sgub_ref, convw_ref, gg_ref, wout_ref, o_ref, p_scr, a_scr, s2_scr, s4_scr,
                  s8_scr, z_scr, y_scr, winb_scr, woutb_scr):
    @pl.when(pl.program_id(0) == 0)
    def _():
        winb_scr[...] = win_ref[...].astype(BF16)
        woutb_scr[...] = wout_ref[...].astype(BF16)

    tiles_per_seq = SEQ // TM_MIX
    seq_tile = pl.program_id(0) % tiles_per_seq

    @pl.when(seq_tile == 0)
    def _():
        a_scr[0:POOL_HALO, :] = jnp.zeros((POOL_HALO, D_POOL), F32)
        z_scr[0:CONV_HALO, :] = jnp.zeros((CONV_HALO, D_CONV), F32)

    x = x_ref[...]
    hb = _rms(x, _layer_row(n1g_ref, layer)).astype(BF16)

    p_scr[...] = jnp.dot(hb, winb_scr[...], preferred_element_type=F32)

    def proj(c0, width):
        return p_scr[:, c0:c0 + width]

    a = proj(C_A, D_POOL)
    rows = POOL_HALO + TM_MIX
    a_scr[POOL_HALO:rows, :] = a
    levels = (a_scr, s2_scr, s4_scr, s8_scr)
    for k in range(1, len(POOL_WINDOWS)):
        w, lo = POOL_WINDOWS[k - 1] // 2, 8 * k
        prev = levels[k - 1]
        levels[k][lo:rows, :] = prev[lo:rows, :] + prev[lo - w:rows - w, :]
    w_last = POOL_WINDOWS[-1] // 2
    s_last = s8_scr[POOL_HALO:rows, :] + s8_scr[POOL_HALO - w_last:rows - w_last, :]
    lane = lax.broadcasted_iota(jnp.int32, (1, D_POOL), 1)
    group = lane // POOL_GC
    wsum = jnp.where(group == 0, s2_scr[POOL_HALO:rows, :],
                     jnp.where(group == 1, s4_scr[POOL_HALO:rows, :],
                               jnp.where(group == 2, s8_scr[POOL_HALO:rows, :], s_last)))
    wlane = jnp.where(group == 0, POOL_WINDOWS[0],
                      jnp.where(group == 1, POOL_WINDOWS[1],
                                jnp.where(group == 2, POOL_WINDOWS[2], POOL_WINDOWS[3])))
    seen = seq_tile * TM_MIX + lax.broadcasted_iota(jnp.int32, (TM_MIX, 1), 0) + 1
    inv_count = jnp.where(seen >= wlane, 1.0 / wlane.astype(F32), 1.0 / seen.astype(F32))
    d = wsum * inv_count - a
    y_a = jnp.dot(d.astype(BF16), poolw_ref[...], preferred_element_type=F32)
    y_a = y_a * _layer_row(pscale_ref, layer)
    a_scr[0:POOL_HALO, :] = a_scr[TM_MIX:rows, :]
    gg = _layer_row(gg_ref, layer)
    y_scr[:, 0:D_POOL] = _rms(y_a, gg[:, 0:D_POOL]).astype(BF16)

    gb = proj(C_GB, D_CONV)
    z = proj(C_GC, D_CONV) * proj(C_XC, D_CONV)
    z_scr[CONV_HALO:CONV_HALO + TM_MIX, :] = z
    cw = convw_ref[...]
    zc = (cw[0:1, :] * z_scr[CONV_HALO - 2:CONV_HALO - 2 + TM_MIX, :]
          + cw[1:2, :] * z_scr[CONV_HALO - 1:CONV_HALO - 1 + TM_MIX, :]
          + cw[2:3, :] * z)
    y_c = gb * zc
    z_scr[0:CONV_HALO, :] = z_scr[TM_MIX:TM_MIX + CONV_HALO, :]
    y_scr[:, D_POOL + D_SGU:] = _rms(y_c, gg[:, D_POOL + D_SGU:]).astype(BF16)

    u = proj(C_U, D_SGU)
    v = proj(C_V, D_SGU)
    mu = jnp.mean(v, axis=-1, keepdims=True)
    vc = v - mu
    var = jnp.mean(vc * vc, axis=-1, keepdims=True)
    vn = ((vc * lax.rsqrt(var + EPS)) * _layer_row(lng_ref, layer)
          + _layer_row(lnb_ref, layer)).astype(BF16)
    ci = lax.broadcasted_iota(jnp.int32, (SGU_BLOCK, SGU_BLOCK), 0) // CHUNK
    cj = lax.broadcasted_iota(jnp.int32, (SGU_BLOCK, SGU_BLOCK), 1) // CHUNK
    mask = (ci >= cj).astype(F32)
    sgub = sgub_ref[...]
    head_cols = []
    for hd in range(SGU_HEADS):
        wm = (sguw_ref[hd] * mask).astype(BF16)
        bias = sgub[:, hd:hd + 1]
        blocks = []
        for blk in range(TM_MIX // SGU_BLOCK):
            vblk = vn[blk * SGU_BLOCK:(blk + 1) * SGU_BLOCK, hd * SGU_HD:(hd + 1) * SGU_HD]
            blocks.append(jnp.dot(wm, vblk, preferred_element_type=F32) + bias)
        head_cols.append(jnp.concatenate(blocks, axis=0))
    mixed = jnp.concatenate(head_cols, axis=1)
    y_b = u * mixed
    y_scr[:, D_POOL:D_POOL + D_SGU] = _rms(y_b, gg[:, D_POOL:D_POOL + D_SGU]).astype(BF16)

    o_ref[...] = x + jnp.dot(y_scr[...], woutb_scr[...], preferred_element_type=F32)


def _token_mixer(layer, x2d, n1g, win, poolw_bd, pscale, lng, lnb, sguw, sgub_t, convw, gg, wout):
    n_tok = x2d.shape[0]
    vec = lambda n: pl.BlockSpec((DEPTH, n), lambda i: (0, 0))
    mat = lambda *shape: pl.BlockSpec((None,) + shape, lambda i: (layer,) + (0,) * len(shape))
    return pl.pallas_call(
        functools.partial(_mixer_kernel, layer),
        grid=(n_tok // TM_MIX,),
        in_specs=[
            pl.BlockSpec((TM_MIX, D_MODEL), lambda i: (i, 0)),
            vec(D_MODEL),
            mat(D_MODEL, D_IN),
            mat(D_POOL, D_POOL),
            vec(D_POOL),
            vec(D_SGU),
            vec(D_SGU),
            mat(SGU_HEADS, SGU_BLOCK, SGU_BLOCK),
            mat(SGU_BLOCK, SGU_HEADS),
            mat(CONV_W, D_CONV),
            vec(D_MODEL),
            mat(D_MODEL, D_MODEL),
        ],
        out_specs=pl.BlockSpec((TM_MIX, D_MODEL), lambda i: (i, 0)),
        out_shape=jax.ShapeDtypeStruct((n_tok, D_MODEL), F32),
        scratch_shapes=[
            pltpu.VMEM((TM_MIX, D_IN), F32),
            pltpu.VMEM((POOL_HALO + TM_MIX, D_POOL), F32),
            pltpu.VMEM((POOL_HALO + TM_MIX, D_POOL), F32),
            pltpu.VMEM((POOL_HALO + TM_MIX, D_POOL), F32),
            pltpu.VMEM((POOL_HALO + TM_MIX, D_POOL), F32),
            pltpu.VMEM((CONV_HALO + TM_MIX, D_CONV), F32),
            pltpu.VMEM((TM_MIX, D_MODEL), BF16),
            pltpu.VMEM((D_MODEL, D_IN), BF16),
            pltpu.VMEM((D_MODEL, D_MODEL), BF16),
        ],
        compiler_params=pltpu.CompilerParams(
            dimension_semantics=("arbitrary",), vmem_limit_bytes=VMEM_LIMIT),
        name="token_mixer",
    )(x2d, n1g, win, poolw_bd, pscale, lng, lnb, sguw, sgub_t, convw, gg, wout)


def _swiglu_step(hb, wg_ref, wu_ref, wd_ref):
    g = jnp.dot(hb, wg_ref[...].astype(BF16), preferred_element_type=F32)
    u = jnp.dot(hb, wu_ref[...].astype(BF16), preferred_element_type=F32)
    act = (g * jax.nn.sigmoid(g)) * u
    return jnp.dot(act.astype(BF16), wd_ref[...].astype(BF16), preferred_element_type=F32)


def _ffn_kernel(layer, x_ref, n2g_ref, wg_ref, wu_ref, wd_ref, o_ref, hb_scr):
    @pl.when(pl.program_id(1) == 0)
    def _():
        x = x_ref[...]
        hb_scr[...] = _rms(x, _layer_row(n2g_ref, layer)).astype(BF16)
        o_ref[...] = x

    o_ref[...] += _swiglu_step(hb_scr[...], wg_ref, wu_ref, wd_ref)


def _dense_ffn(layer, x2d, n2g, wg, wu, wd):
    n_tok = x2d.shape[0]
    return pl.pallas_call(
        functools.partial(_ffn_kernel, layer),
        grid=(n_tok // TM_FFN, D_FF // TF_FFN),
        in_specs=[
            pl.BlockSpec((TM_FFN, D_MODEL), lambda i, j: (i, 0)),
            pl.BlockSpec((DEPTH, D_MODEL), lambda i, j: (0, 0)),
            pl.BlockSpec((None, D_MODEL, TF_FFN), lambda i, j: (layer // 2, 0, j)),
            pl.BlockSpec((None, D_MODEL, TF_FFN), lambda i, j: (layer // 2, 0, j)),
            pl.BlockSpec((None, TF_FFN, D_MODEL), lambda i, j: (layer // 2, j, 0)),
        ],
        out_specs=pl.BlockSpec((TM_FFN, D_MODEL), lambda i, j: (i, 0)),
        out_shape=jax.ShapeDtypeStruct((n_tok, D_MODEL), F32),
        scratch_shapes=[pltpu.VMEM((TM_FFN, D_MODEL), BF16)],
        compiler_params=pltpu.CompilerParams(
            dimension_semantics=("arbitrary", "arbitrary"), vmem_limit_bytes=VMEM_LIMIT),
        name="dense_ffn",
    )(x2d, n2g, wg, wu, wd)


def _router_kernel(layer, x_ref, n2g_ref, rw_ref, rb_ref, h_ref, meta_ref, meta_t_ref, cnt_ref):
    @pl.when(pl.program_id(0) == 0)
    def _():
        cnt_ref[...] = jnp.zeros_like(cnt_ref)

    h = _rms(x_ref[...], _layer_row(n2g_ref, layer))
    h_ref[...] = h
    hb = h.astype(BF16)
    logits = jnp.dot(hb, rw_ref[...], preferred_element_type=F32) + rb_ref[...]
    lane = lax.broadcasted_iota(jnp.int32, logits.shape, 1).astype(F32)
    m1 = jnp.max(logits, axis=-1, keepdims=True)
    i1 = jnp.min(jnp.where(logits == m1, lane, float(LANES)), axis=-1, keepdims=True)
    rest = jnp.where(lane == i1, -jnp.inf, logits)
    m2 = jnp.max(rest, axis=-1, keepdims=True)
    i2 = jnp.min(jnp.where(rest == m2, lane, float(LANES)), axis=-1, keepdims=True)
    e2 = jnp.exp(m2 - m1)
    denom = 1.0 + e2

    sel = jnp.where((lane == i1) | (lane == i2), 1.0, 0.0)
    r = lax.broadcasted_iota(jnp.int32, (TM_ROUTE, TM_ROUTE), 0)
    c = lax.broadcasted_iota(jnp.int32, (TM_ROUTE, TM_ROUTE), 1)
    earlier = jnp.where(c < r, 1.0, 0.0).astype(BF16)
    before = jnp.dot(earlier, sel.astype(BF16), preferred_element_type=F32) + cnt_ref[...]
    rank1 = jnp.sum(jnp.where(lane == i1, before, 0.0), axis=-1, keepdims=True)
    rank2 = jnp.sum(jnp.where(lane == i2, before, 0.0), axis=-1, keepdims=True)
    cnt_ref[...] += jnp.sum(sel, axis=0, keepdims=True)

    record = jnp.zeros_like(logits)
    for k, val in ((M_I1, i1), (M_I2, i2), (M_R1, rank1), (M_R2, rank2),
                   (M_G1, 1.0 / denom), (M_G2, e2 / denom)):
        record = jnp.where(lane == k, val, record)
    meta_ref[...] = record
    meta_t_ref[...] = record.T[0:META_ROWS, :]


def _router(layer, x2d, n2g, rw_pad, rb_pad):
    n_tok = x2d.shape[0]
    return pl.pallas_call(
        functools.partial(_router_kernel, layer),
        grid=(n_tok // TM_ROUTE,),
        in_specs=[
            pl.BlockSpec((TM_ROUTE, D_MODEL), lambda i: (i, 0)),
            pl.BlockSpec((DEPTH, D_MODEL), lambda i: (0, 0)),
            pl.BlockSpec((D_MODEL, LANES), lambda i: (0, 0)),
            pl.BlockSpec((1, LANES), lambda i: (0, 0)),
        ],
        out_specs=[
            pl.BlockSpec((TM_ROUTE, D_MODEL), lambda i: (i, 0)),
            pl.BlockSpec((TM_ROUTE, LANES), lambda i: (i, 0)),
            pl.BlockSpec((META_ROWS, TM_ROUTE), lambda i: (0, i)),
            pl.BlockSpec((1, LANES), lambda i: (0, 0)),
        ],
        out_shape=[
            jax.ShapeDtypeStruct((n_tok, D_MODEL), F32),
            jax.ShapeDtypeStruct((n_tok, LANES), F32),
            jax.ShapeDtypeStruct((META_ROWS, n_tok), F32),
            jax.ShapeDtypeStruct((1, LANES), F32),
        ],
        compiler_params=pltpu.CompilerParams(
            dimension_semantics=("arbitrary",), vmem_limit_bytes=VMEM_LIMIT),
        name="router",
    )(x2d, n2g, rw_pad, rb_pad)


SUBLANES = 8
TILES_MOVE = TM_MOVE // SUBLANES


def _issue_row_copies(make_copy):
    def issue(tile, carry):
        for sub in range(SUBLANES):
            for k in range(TOP_K):
                make_copy(tile, sub, tile * SUBLANES + sub, k).start(priority=k)
        return carry

    lax.fori_loop(0, TILES_MOVE, issue, 0)


def _dispatch_kernel(zstart_ref, zflag_ref, pos1_ref, pos2_ref, x_ref, xs_ref, x_scr, zero_scr,
                     sem_in, sem_rows, sem_fill):
    pos_refs = (pos1_ref, pos2_ref)
    step = pl.program_id(0)
    cur = step % 2

    def tile_load(tile, b):
        start = tile * TILES_MOVE
        return pltpu.make_async_copy(x_ref.at[pl.ds(start, TILES_MOVE)], x_scr.at[b], sem_in.at[b])

    def wait_rows(b):
        for k in range(TOP_K):
            pltpu.make_async_copy(x_scr.at[b], x_scr.at[b], sem_rows.at[b, k]).wait()

    @pl.when(step == 0)
    def _():
        tile_load(0, 0).start()
        zero_scr[...] = jnp.zeros_like(zero_scr)
        for e in range(2 * N_EXPERTS):
            @pl.when(zflag_ref[e] > 0)
            def _():
                start = pl.multiple_of(zstart_ref[e], TQ_EXP)
                fill = pltpu.make_async_copy(zero_scr, xs_ref.at[pl.ds(start, TQ_EXP)], sem_fill)
                fill.start()
                fill.wait()

    @pl.when(step > 0)
    def _():
        wait_rows(1 - cur)

    @pl.when(step + 1 < pl.num_programs(0))
    def _():
        tile_load(step + 1, 1 - cur).start()

    tile_load(step, cur).wait()

    _issue_row_copies(lambda tile, sub, row, k: pltpu.make_async_copy(
        x_scr.at[cur, tile, pl.ds(sub, 1)], xs_ref.at[pl.ds(pos_refs[k][row], 1)],
        sem_rows.at[cur, k]))

    @pl.when(step == pl.num_programs(0) - 1)
    def _():
        wait_rows(cur)


def _dispatch(x2d, pos, zstart, zflag, n_rows):
    n_tok = x2d.shape[0]
    grid_spec = pltpu.PrefetchScalarGridSpec(
        num_scalar_prefetch=2,
        grid=(n_tok // TM_MOVE,),
        in_specs=[
            pl.BlockSpec((TM_MOVE,), lambda i, zs, zf: (i,), memory_space=pltpu.SMEM),
            pl.BlockSpec((TM_MOVE,), lambda i, zs, zf: (i,), memory_space=pltpu.SMEM),
            pl.BlockSpec(memory_space=pl.ANY),
        ],
        out_specs=pl.BlockSpec(memory_space=pl.ANY),
        scratch_shapes=[pltpu.VMEM((2, TILES_MOVE, SUBLANES, D_MODEL), F32),
                        pltpu.VMEM((TQ_EXP, D_MODEL), F32),
                        pltpu.SemaphoreType.DMA((2,)), pltpu.SemaphoreType.DMA((2, TOP_K)),
                        pltpu.SemaphoreType.DMA(())],
    )
    return pl.pallas_call(
        _dispatch_kernel,
        grid_spec=grid_spec,
        out_shape=jax.ShapeDtypeStruct((n_rows, D_MODEL), F32),
        compiler_params=pltpu.CompilerParams(
            dimension_semantics=("arbitrary",), vmem_limit_bytes=VMEM_LIMIT),
        name="dispatch",
    )(zstart, zflag, pos[0], pos[1], x2d.reshape(n_tok // SUBLANES, SUBLANES, D_MODEL))


def _expert_kernel(se_ref, sq_ref, sn_ref, nv_ref, *refs):
    xq_refs = refs[:NQ_EXP]
    wg_ref, wu_ref, wd_ref, y_ref, hb_scr, wgb_scr, wub_scr, wdb_scr = refs[NQ_EXP:]
    step = pl.program_id(0)
    n_quarters = sn_ref[step]
    valid = step < nv_ref[0]

    @pl.when(pl.program_id(1) == 0)
    def _():
        for u, xq_ref in enumerate(xq_refs):
            hb_scr[u * TQ_EXP:(u + 1) * TQ_EXP, :] = xq_ref[...].astype(BF16)
        y_ref[...] = jnp.zeros_like(y_ref)

    @pl.when(valid & (n_quarters == NQ_EXP))
    def _():
        y_ref[...] += _swiglu_step(hb_scr[...], wg_ref.at[0], wu_ref.at[0], wd_ref.at[0])

    @pl.when(valid & (n_quarters < NQ_EXP))
    def _():
        wgb_scr[...] = wg_ref[0].astype(BF16)
        wub_scr[...] = wu_ref[0].astype(BF16)
        wdb_scr[...] = wd_ref[0].astype(BF16)
        for u in range(NQ_EXP - 1):
            @pl.when(u < n_quarters)
            def _():
                quarter = slice(u * TQ_EXP, (u + 1) * TQ_EXP)
                y_ref[quarter, :] += _swiglu_step(hb_scr[quarter, :], wgb_scr, wub_scr, wdb_scr)


def _expert_ffn(layer, xs, wg, wu, wd, step_expert, step_quarter, step_count, n_valid):
    n_steps = step_expert.shape[0]
    n_j = D_FF // TF_FFN
    moe = layer // 2

    def ff_block(i, j, nv):
        return jnp.where(i < nv[0], j, n_j - 1)

    def quarter_spec(u):
        return pl.BlockSpec(
            (TQ_EXP, D_MODEL),
            lambda i, j, se, sq, sn, nv: (sq[i] + jnp.minimum(u, sn[i] - 1), 0))

    grid_spec = pltpu.PrefetchScalarGridSpec(
        num_scalar_prefetch=4,
        grid=(n_steps, n_j),
        in_specs=[quarter_spec(u) for u in range(NQ_EXP)] + [
            pl.BlockSpec((None, 1, D_MODEL, TF_FFN),
                         lambda i, j, se, sq, sn, nv: (moe, se[i], 0, ff_block(i, j, nv))),
            pl.BlockSpec((None, 1, D_MODEL, TF_FFN),
                         lambda i, j, se, sq, sn, nv: (moe, se[i], 0, ff_block(i, j, nv))),
            pl.BlockSpec((None, 1, TF_FFN, D_MODEL),
                         lambda i, j, se, sq, sn, nv: (moe, se[i], ff_block(i, j, nv), 0)),
        ],
        out_specs=pl.BlockSpec((TM_EXP, D_MODEL), lambda i, j, se, sq, sn, nv: (i, 0)),
        scratch_shapes=[pltpu.VMEM((TM_EXP, D_MODEL), BF16),
                        pltpu.VMEM((D_MODEL, TF_FFN), BF16), pltpu.VMEM((D_MODEL, TF_FFN), BF16),
                        pltpu.VMEM((TF_FFN, D_MODEL), BF16)],
    )
    return pl.pallas_call(
        _expert_kernel,
        grid_spec=grid_spec,
        out_shape=jax.ShapeDtypeStruct((n_steps * TM_EXP, D_MODEL), F32),
        compiler_params=pltpu.CompilerParams(
            dimension_semantics=("arbitrary", "arbitrary"), vmem_limit_bytes=VMEM_LIMIT),
        name="expert_ffn",
    )(step_expert, step_quarter, step_count, n_valid, *([xs] * NQ_EXP), wg, wu, wd)


def _combine_kernel(pos1_ref, pos2_ref, pos1_next_ref, pos2_next_ref, x_ref, meta_ref, fg_ref,
                    y_ref, o_ref, y_scr, sem):
    step = pl.program_id(0)
    cur = step % 2

    def gather(pos_refs, b):
        _issue_row_copies(lambda tile, sub, row, k: pltpu.make_async_copy(
            y_ref.at[pl.ds(pos_refs[k][row], 1)], y_scr.at[b, k, tile, pl.ds(sub, 1)],
            sem.at[b, k]))

    @pl.when(step == 0)
    def _():
        gather((pos1_ref, pos2_ref), 0)

    @pl.when(step + 1 < pl.num_programs(0))
    def _():
        gather((pos1_next_ref, pos2_next_ref), 1 - cur)

    for k in range(TOP_K):
        pltpu.make_async_copy(y_scr.at[cur, k], y_scr.at[cur, k], sem.at[cur, k]).wait()

    meta = meta_ref[...]
    rows = lambda k: y_scr[cur, k].reshape(TM_MOVE, D_MODEL)
    moe = meta[:, M_G1:M_G1 + 1] * rows(0) + meta[:, M_G2:M_G2 + 1] * rows(1)
    o_ref[...] = _rms(x_ref[...] + moe, fg_ref[...])


def _combine(x2d, meta, fg, y, pos):
    n_tok = x2d.shape[0]
    n_steps = n_tok // TM_MOVE
    next_block = lambda i: (jnp.minimum(i + 1, n_steps - 1),)
    return pl.pallas_call(
        _combine_kernel,
        grid=(n_steps,),
        in_specs=[
            pl.BlockSpec((TM_MOVE,), lambda i: (i,), memory_space=pltpu.SMEM),
            pl.BlockSpec((TM_MOVE,), lambda i: (i,), memory_space=pltpu.SMEM),
            pl.BlockSpec((TM_MOVE,), next_block, memory_space=pltpu.SMEM),
            pl.BlockSpec((TM_MOVE,), next_block, memory_space=pltpu.SMEM),
            pl.BlockSpec((TM_MOVE, D_MODEL), lambda i: (i, 0)),
            pl.BlockSpec((TM_MOVE, LANES), lambda i: (i, 0)),
            pl.BlockSpec((1, D_MODEL), lambda i: (0, 0)),
            pl.BlockSpec(memory_space=pl.ANY),
        ],
        out_specs=pl.BlockSpec((TM_MOVE, D_MODEL), lambda i: (i, 0)),
        out_shape=jax.ShapeDtypeStruct((n_tok, D_MODEL), F32),
        scratch_shapes=[pltpu.VMEM((2, TOP_K, TILES_MOVE, SUBLANES, D_MODEL), F32),
                        pltpu.SemaphoreType.DMA((2, TOP_K))],
        compiler_params=pltpu.CompilerParams(
            dimension_semantics=("arbitrary",), vmem_limit_bytes=VMEM_LIMIT),
        name="combine",
    )(pos[0], pos[1], pos[0], pos[1], x2d, meta, fg, y)


def _routing_tables(meta_t, counts_f, n_steps, n_quarters):
    i32 = jnp.int32
    experts = jnp.arange(N_EXPERTS, dtype=i32)
    counts = counts_f[0, :N_EXPERTS].astype(i32)
    quarters = (counts + TQ_EXP - 1) // TQ_EXP
    q_end = jnp.cumsum(quarters)
    q_start = q_end - quarters
    steps = (quarters + NQ_EXP - 1) // NQ_EXP
    s_end = jnp.cumsum(steps)
    s_start = s_end - steps
    n_valid = s_end[-1]

    idx = meta_t[M_I1:M_I2 + 1].astype(i32)
    rank = meta_t[M_R1:M_R2 + 1].astype(i32)
    slot_base = jnp.zeros_like(rank)
    step_base = jnp.zeros_like(rank)
    for e in range(N_EXPERTS):
        slot_base = slot_base + jnp.where(idx == e, q_start[e] * TQ_EXP, 0)
        step_base = step_base + jnp.where(idx == e, s_start[e], 0)
    slot = slot_base + rank
    local_q = rank // TQ_EXP
    y_row = (step_base + local_q // NQ_EXP) * TM_EXP + rank % TM_EXP

    step = jnp.minimum(jnp.arange(n_steps, dtype=i32), n_valid - 1)
    step_expert = jnp.minimum(jnp.sum(s_end[None, :] <= step[:, None], axis=-1), N_EXPERTS - 1)
    pick = lambda table: jnp.sum(jnp.where(step_expert[:, None] == experts, table, 0), axis=-1)
    local_step = step - pick(s_start)
    step_quarter = pick(q_start) + NQ_EXP * local_step
    step_count = jnp.minimum(NQ_EXP, pick(quarters) - NQ_EXP * local_step)

    tail = jnp.arange(n_quarters - N_EXPERTS, n_quarters, dtype=i32)
    zstart = jnp.concatenate([(q_end - 1) * TQ_EXP, tail * TQ_EXP])
    zflag = jnp.concatenate([quarters > 0, tail >= q_end[-1]])
    return (slot.astype(i32), y_row.astype(i32), step_expert.astype(i32), step_quarter.astype(i32),
            step_count.astype(i32), n_valid.reshape(1).astype(i32), zstart.astype(i32),
            zflag.astype(i32))


def _block_diag(blocks):
    layers, n, r, c = blocks.shape
    eye = jnp.eye(n, dtype=blocks.dtype)
    return (eye[None, :, None, :, None] * blocks[:, :, :, None, :]).reshape(layers, n * r, n * c)


def kernel(x, norm1_g, w_in, pool_w, pool_scale, sgu_ln_g, sgu_ln_b, sgu_w, sgu_b, conv_w,
           group_g, w_out, norm2_g, ffn_w_gate, ffn_w_up, ffn_w_down, router_w, router_b,
           moe_w_gate, moe_w_up, moe_w_down, final_g):
    bsz, seq, d = x.shape
    assert (seq, d) == (SEQ, D_MODEL) and DEPTH == 2
    x2d = x.reshape(bsz * seq, d)
    mixer_params = (norm1_g, w_in, _block_diag(pool_w).astype(BF16), pool_scale,
                    sgu_ln_g, sgu_ln_b, sgu_w, sgu_b.transpose(0, 2, 1), conv_w, group_g,
                    w_out)

    x2d = _token_mixer(0, x2d, *mixer_params)
    x2d = _dense_ffn(0, x2d, norm2_g, ffn_w_gate, ffn_w_up, ffn_w_down)
    x2d = _token_mixer(1, x2d, *mixer_params)
    rw_pad = jnp.zeros((d, LANES), BF16).at[:, :N_EXPERTS].set(router_w[0].astype(BF16))
    rb_pad = jnp.full((1, LANES), -1e30, F32).at[0, :N_EXPERTS].set(router_b[0])
    h, meta, meta_t, counts = _router(1, x2d, norm2_g, rw_pad, rb_pad)
    n_steps = (TOP_K * bsz * seq) // TM_EXP + N_EXPERTS
    n_quarters = (TOP_K * bsz * seq) // TQ_EXP + N_EXPERTS
    slot, y_row, step_expert, step_quarter, step_count, n_valid, zstart, zflag = _routing_tables(
        meta_t, counts, n_steps, n_quarters)
    xs = _dispatch(h, slot, zstart, zflag, n_quarters * TQ_EXP)
    y = _expert_ffn(1, xs, moe_w_gate, moe_w_up, moe_w_down, step_expert, step_quarter,
                    step_count, n_valid)
    out = _combine(x2d, meta, final_g.reshape(1, d), y, y_row)
    return out.reshape(bsz, seq, d)
```
